```python
import math
import jax, jax.numpy as jnp
from jax import lax
import numpy as np

D_MODEL = 1024
BATCH = 2
SEQ = 8192
DEPTH = 4
DEC_BATCH = 128
DEC_SEQ = 4
PAST_LEN = 8192
PAGE_SIZE = 128

LRU_WIDTH = D_MODEL
LRU_BLOCKS = 16
LRU_BLOCK = LRU_WIDTH // LRU_BLOCKS
CONV_WIDTH = 4
LRU_C = 8.0
SB_HEADS = 8
SB_KV_HEADS = 2
SB_GROUP = SB_HEADS // SB_KV_HEADS
SB_HEAD_DIM = 64
SB_SCALE = SB_HEAD_DIM ** -0.5
MLA_HEADS = 8
MLA_Q_RANK = 256
MLA_KV_RANK = 128
MLA_NOPE = 64
MLA_ROPE = 32
MLA_V = 64
MLA_SCALE = (MLA_NOPE + MLA_ROPE) ** -0.5
ROPE_THETA = 10000.0
MEM_LEN = 256
MEM_HEADS = 4
MEM_HEAD_DIM = 128
MEM_SCALE = MEM_HEAD_DIM ** -0.5
D_FF = 4 * D_MODEL
N_BRANCH = 3
Q_BLOCK = 128
EPS = 1e-6
IN_SPLITS = (LRU_WIDTH, LRU_WIDTH, SB_HEADS * SB_HEAD_DIM, SB_KV_HEADS * SB_HEAD_DIM,
             SB_KV_HEADS * SB_HEAD_DIM, MLA_Q_RANK, MLA_KV_RANK, MLA_ROPE, N_BRANCH * D_MODEL)
D_IN = (2 * LRU_WIDTH + SB_HEADS * SB_HEAD_DIM + 2 * SB_KV_HEADS * SB_HEAD_DIM
        + MLA_Q_RANK + MLA_KV_RANK + MLA_ROPE + N_BRANCH * D_MODEL)

kernel_name = 'hawk_stickbreak_mla_gated_hybrid_step'


def rmsnorm(x, g):
    xf = x.astype(jnp.float32)
    y = xf * lax.rsqrt(jnp.mean(xf * xf, axis=-1, keepdims=True) + EPS)
    return (y * g.astype(jnp.float32)).astype(x.dtype)


def split_cols(a, sizes):
    out, off = [], 0
    for s in sizes:
        out.append(a[..., off:off + s])
        off += s
    return out


def rope(x, pos):
    half = x.shape[-1] // 2
    inv = ROPE_THETA ** (-jnp.arange(half, dtype=jnp.float32) / half)
    ang = pos.astype(jnp.float32)[:, None] * inv[None, :]
    shp = (ang.shape[0],) + (1,) * (x.ndim - 3) + (half,)
    cos, sin = jnp.cos(ang).reshape(shp), jnp.sin(ang).reshape(shp)
    xf = x.astype(jnp.float32)
    x1, x2 = xf[..., :half], xf[..., half:]
    return jnp.concatenate([x1 * cos - x2 * sin, x2 * cos + x1 * sin], axis=-1).astype(x.dtype)


def causal_conv(x, buf, w, b):
    T = x.shape[1]
    xp = jnp.concatenate([buf, x], axis=1)
    y = b
    for k in range(CONV_WIDTH):
        y = y + xp[:, k:k + T] * w[k]
    return y, xp[:, xp.shape[1] - (CONV_WIDTH - 1):]


def rg_lru(x, h0, pos, wa, ba, wx, bx, lam):
    B, T, _ = x.shape
    xb = x.reshape(B, T, LRU_BLOCKS, LRU_BLOCK)
    r = jax.nn.sigmoid(jnp.einsum('btnc,ncd->btnd', xb, wa).reshape(B, T, LRU_WIDTH) + ba)
    i = jax.nn.sigmoid(jnp.einsum('btnc,ncd->btnd', xb, wx).reshape(B, T, LRU_WIDTH) + bx)
    log_a = -LRU_C * r.astype(jnp.float32) * jax.nn.softplus(-lam.astype(jnp.float32))
    a = jnp.exp(log_a)
    mult = jnp.sqrt(jnp.maximum(-jnp.expm1(2.0 * log_a), 0.0))
    mult = jnp.where((pos == 0)[None, :, None], 1.0, mult)
    u = mult * (i * x).astype(jnp.float32)

    def combine(left, right):
        a1, b1 = left
        a2, b2 = right
        return a1 * a2, a2 * b1 + b2

    a_cum, b_cum = lax.associative_scan(combine, (a, u), axis=1)
    h = a_cum * h0.astype(jnp.float32)[:, None, :] + b_cum
    return h.astype(x.dtype), h[:, -1].astype(h0.dtype)


def sb_attend(q, q_pos, k, v, k_pos):
    z = jnp.einsum('bqhgd,bkhd->bhgqk', q, k, preferred_element_type=jnp.float32) * SB_SCALE
    mask = k_pos[None, :] < q_pos[:, None]
    log_keep = jnp.where(mask, jax.nn.log_sigmoid(-z), 0.0)
    suffix = lax.cumsum(log_keep, axis=4, reverse=True)
    w = jnp.exp(jnp.where(mask, z + suffix, -jnp.inf))
    return jnp.einsum('bhgqk,bkhd->bqhgd', w.astype(v.dtype), v)


def mla_attend(q_abs, q_rope, q_pos, c_kv, k_rope, k_pos):
    s = (jnp.einsum('bqhc,bkc->bhqk', q_abs, c_kv, preferred_element_type=jnp.float32)
         + jnp.einsum('bqhr,bkr->bhqk', q_rope, k_rope, preferred_element_type=jnp.float32)) * MLA_SCALE
    mask = k_pos[None, :] <= q_pos[:, None]
    p = jax.nn.softmax(jnp.where(mask, s, -jnp.inf), axis=-1)
    return jnp.einsum('bhqk,bkc->bqhc', p.astype(c_kv.dtype), c_kv)


def mem_attend(q, mk, mv):
    s = jnp.einsum('bqhd,bkhd->bhqk', q, mk, preferred_element_type=jnp.float32) * MEM_SCALE
    p = jax.nn.softmax(s, axis=-1)
    return jnp.einsum('bhqk,bkhd->bqhd', p.astype(mv.dtype), mv)


def sweep_query_blocks(fn, qs, q_pos, kv):
    T = q_pos.shape[0]
    if T <= Q_BLOCK or T % Q_BLOCK != 0:
        return fn(*qs, q_pos, *kv)
    nb = T // Q_BLOCK
    qb = tuple(jnp.moveaxis(q.reshape((q.shape[0], nb, Q_BLOCK) + q.shape[2:]), 1, 0) for q in qs)
    pb = q_pos.reshape(nb, Q_BLOCK)
    out = lax.map(lambda a: fn(*a[0], a[1], *kv), (qb, pb))
    out = jnp.moveaxis(out, 0, 1)
    return out.reshape((out.shape[0], T) + out.shape[3:])


def gather_past(cache, l, page_table):
    g = cache[l, page_table]
    return g.reshape((g.shape[0], g.shape[1] * g.shape[2]) + g.shape[3:])


def trunk_layer(x, lru_h0, conv_buf, sb_k_past, sb_v_past, lat_past, kr_past, mem_k, mem_v,
                g_mix, w_in, b_gate, conv_w, conv_b, lru_wa, lru_ba, lru_wx, lru_bx, lru_lambda,
                mla_gq, mla_wuq, mla_gkv, mla_wuk, mla_wuv, w_lru_o, w_sb_o, w_mla_o, w_out,
                g_mem, w_mq, w_mo, g_mlp, w_up, w_down):
    B, T, _ = x.shape
    P = sb_k_past.shape[1]
    pos = P + jnp.arange(T, dtype=jnp.int32)
    k_pos = jnp.arange(P + T, dtype=jnp.int32)
    xn = rmsnorm(x, g_mix)
    u_lru, u_gelu, u_q, u_k, u_v, u_cq, u_ckv, u_kr, u_gate = split_cols(xn @ w_in, IN_SPLITS)

    xc, conv_new = causal_conv(u_lru, conv_buf, conv_w, conv_b)
    h_seq, h_last = rg_lru(xc, lru_h0, pos, lru_wa, lru_ba, lru_wx, lru_bx, lru_lambda)
    y_a = (h_seq * jax.nn.gelu(u_gelu)) @ w_lru_o

    q = u_q.reshape(B, T, SB_KV_HEADS, SB_GROUP, SB_HEAD_DIM)
    k_new = u_k.reshape(B, T, SB_KV_HEADS, SB_HEAD_DIM)
    v_new = u_v.reshape(B, T, SB_KV_HEADS, SB_HEAD_DIM)
    k_all = jnp.concatenate([sb_k_past, k_new], axis=1)
    v_all = jnp.concatenate([sb_v_past, v_new], axis=1)
    o_b = sweep_query_blocks(sb_attend, (q,), pos, (k_all, v_all, k_pos))
    y_b = o_b.reshape(B, T, SB_HEADS * SB_HEAD_DIM) @ w_sb_o

    qc = (rmsnorm(u_cq, mla_gq) @ mla_wuq).reshape(B, T, MLA_HEADS, MLA_NOPE + MLA_ROPE)
    q_abs = jnp.einsum('bthd,chd->bthc', qc[..., :MLA_NOPE], mla_wuk)
    q_rope = rope(qc[..., MLA_NOPE:], pos)
    lat_new = rmsnorm(u_ckv, mla_gkv)
    kr_new = rope(u_kr, pos)
    lat_all = jnp.concatenate([lat_past, lat_new], axis=1)
    kr_all = jnp.concatenate([kr_past, kr_new], axis=1)
    o_lat = sweep_query_blocks(mla_attend, (q_abs, q_rope), pos, (lat_all, kr_all, k_pos))
    o_c = jnp.einsum('bthc,chd->bthd', o_lat, mla_wuv).reshape(B, T, MLA_HEADS * MLA_V)
    y_c = o_c @ w_mla_o

    gate = jax.nn.sigmoid(u_gate + b_gate).reshape(B, T, N_BRANCH, D_MODEL)
    merged = gate[:, :, 0] * y_a + gate[:, :, 1] * y_b + gate[:, :, 2] * y_c
    x = x + merged @ w_out

    qm = (rmsnorm(x, g_mem) @ w_mq).reshape(B, T, MEM_HEADS, MEM_HEAD_DIM)
    x = x + mem_attend(qm, mem_k, mem_v).reshape(B, T, MEM_HEADS * MEM_HEAD_DIM) @ w_mo

    hdn = jax.nn.relu(rmsnorm(x, g_mlp) @ w_up)
    x = x + (hdn * hdn) @ w_down
    return x, lat_new, kr_new, k_new, v_new, h_last, conv_new


def setup_inputs(seed: int = 0) -> dict:
    key = jax.random.key(seed)
    ks = iter(jax.random.split(key, 64))
    f32 = jnp.float32

    def nrm(shape, scale=1.0):
        return scale * jax.random.normal(next(ks), shape, f32)

    def gain(shape):
        return 1.0 + 0.05 * jax.random.normal(next(ks), shape, f32)

    n_pages = PAST_LEN // PAGE_SIZE
    n_pool = (DEC_BATCH * n_pages * 5) // 4
    perm = jax.random.permutation(next(ks), n_pool)
    page_table = perm[:DEC_BATCH * n_pages].reshape(DEC_BATCH, n_pages).astype(jnp.int32)
    a8 = jax.random.uniform(next(ks), (DEPTH, LRU_WIDTH), f32, 0.9, 0.999)
    a_base = a8 ** (1.0 / LRU_C)
    lru_lambda = jnp.log(a_base) - jnp.log1p(-a_base)
    L = DEPTH
    return {
        'x_prompt': nrm((BATCH, SEQ, D_MODEL)),
        'x_sample': nrm((DEC_BATCH, DEC_SEQ, D_MODEL)),
        'mem_prompt': nrm((BATCH, MEM_LEN, D_MODEL)),
        'cache_mla_latent': nrm((L, n_pool, PAGE_SIZE, MLA_KV_RANK)),
        'cache_mla_krope': nrm((L, n_pool, PAGE_SIZE, MLA_ROPE)),
        'cache_sb_k': nrm((L, n_pool, PAGE_SIZE, SB_KV_HEADS, SB_HEAD_DIM)),
        'cache_sb_v': nrm((L, n_pool, PAGE_SIZE, SB_KV_HEADS, SB_HEAD_DIM)),
        'cache_mem_k': nrm((L, DEC_BATCH, MEM_LEN, MEM_HEADS, MEM_HEAD_DIM)),
        'cache_mem_v': nrm((L, DEC_BATCH, MEM_LEN, MEM_HEADS, MEM_HEAD_DIM)),
        'state_lru_h': nrm((L, DEC_BATCH, LRU_WIDTH), 0.5),
        'state_conv': nrm((L, DEC_BATCH, CONV_WIDTH - 1, LRU_WIDTH)),
        'page_table': page_table,
        'g_mix': gain((L, D_MODEL)),
        'w_in': nrm((L, D_MODEL, D_IN), D_MODEL ** -0.5),
        'b_gate': nrm((L, N_BRANCH * D_MODEL), 0.01),
        'conv_w': nrm((L, CONV_WIDTH, LRU_WIDTH), CONV_WIDTH ** -0.5),
        'conv_b': nrm((L, LRU_WIDTH), 0.01),
        'lru_wa': nrm((L, LRU_BLOCKS, LRU_BLOCK, LRU_BLOCK), LRU_BLOCK ** -0.5),
        'lru_ba': nrm((L, LRU_WIDTH), 0.01),
        'lru_wx': nrm((L, LRU_BLOCKS, LRU_BLOCK, LRU_BLOCK), LRU_BLOCK ** -0.5),
        'lru_bx': nrm((L, LRU_WIDTH), 0.01),
        'lru_lambda': lru_lambda,
        'mla_gq': gain((L, MLA_Q_RANK)),
        'mla_wuq': nrm((L, MLA_Q_RANK, MLA_HEADS * (MLA_NOPE + MLA_ROPE)), MLA_Q_RANK ** -0.5),
        'mla_gkv': gain((L, MLA_KV_RANK)),
        'mla_wuk': nrm((L, MLA_KV_RANK, MLA_HEADS, MLA_NOPE), MLA_KV_RANK ** -0.5),
        'mla_wuv': nrm((L, MLA_KV_RANK, MLA_HEADS, MLA_V), MLA_KV_RANK ** -0.5),
        'w_lru_o': nrm((L, LRU_WIDTH, D_MODEL), LRU_WIDTH ** -0.5),
        'w_sb_o': nrm((L, SB_HEADS * SB_HEAD_DIM, D_MODEL), (SB_HEADS * SB_HEAD_DIM) ** -0.5),
        'w_mla_o': nrm((L, MLA_HEADS * MLA_V, D_MODEL), (MLA_HEADS * MLA_V) ** -0.5),
        'w_out': nrm((L, D_MODEL, D_MODEL), D_MODEL ** -0.5),
        'g_mem': gain((L, D_MODEL)),
        'g_memkv': gain((L, D_MODEL)),
        'w_mq': nrm((L, D_MODEL, MEM_HEADS * MEM_HEAD_DIM), D_MODEL ** -0.5),
        'w_mk': nrm((L, D_MODEL, MEM_HEADS * MEM_HEAD_DIM), D_MODEL ** -0.5),
        'w_mv': nrm((L, D_MODEL, MEM_HEADS * MEM_HEAD_DIM), D_MODEL ** -0.5),
        'w_mo': nrm((L, MEM_HEADS * MEM_HEAD_DIM, D_MODEL), (MEM_HEADS * MEM_HEAD_DIM) ** -0.5),
        'g_mlp': gain((L, D_MODEL)),
        'w_up': nrm((L, D_MODEL, D_FF), D_MODEL ** -0.5),
        'w_down': nrm((L, D_FF, D_MODEL), D_FF ** -0.5),
        'g_final': gain((D_MODEL,)),
    }


def reference(x_prompt, x_sample, mem_prompt, cache_mla_latent, cache_mla_krope, cache_sb_k, cache_sb_v,
              cache_mem_k, cache_mem_v, state_lru_h, state_conv, page_table,
              g_mix, w_in, b_gate, conv_w, conv_b, lru_wa, lru_ba, lru_wx, lru_bx, lru_lambda,
              mla_gq, mla_wuq, mla_gkv, mla_wuk, mla_wuv, w_lru_o, w_sb_o, w_mla_o, w_out,
              g_mem, g_memkv, w_mq, w_mk, w_mv, w_mo, g_mlp, w_up, w_down, g_final):
    Bp = x_prompt.shape[0]
    dt = x_prompt.dtype
    xp, xs = x_prompt, x_sample
    p_lat, p_kr, p_k, p_v, p_h, p_conv, p_mk, p_mv = [], [], [], [], [], [], [], []
    s_lat, s_kr, s_k, s_v, s_h, s_conv = [], [], [], [], [], []
    for l in range(DEPTH):
        lw = (g_mix[l], w_in[l], b_gate[l], conv_w[l], conv_b[l], lru_wa[l], lru_ba[l], lru_wx[l],
              lru_bx[l], lru_lambda[l], mla_gq[l], mla_wuq[l], mla_gkv[l], mla_wuk[l], mla_wuv[l],
              w_lru_o[l], w_sb_o[l], w_mla_o[l], w_out[l], g_mem[l], w_mq[l], w_mo[l],
              g_mlp[l], w_up[l], w_down[l])
        mn = rmsnorm(mem_prompt, g_memkv[l])
        mk = (mn @ w_mk[l]).reshape(Bp, -1, MEM_HEADS, MEM_HEAD_DIM)
        mv = (mn @ w_mv[l]).reshape(Bp, -1, MEM_HEADS, MEM_HEAD_DIM)
        xp, lat, kr, kk, vv, hh, cb = trunk_layer(
            xp, jnp.zeros((Bp, LRU_WIDTH), dt), jnp.zeros((Bp, CONV_WIDTH - 1, LRU_WIDTH), dt),
            jnp.zeros((Bp, 0, SB_KV_HEADS, SB_HEAD_DIM), dt), jnp.zeros((Bp, 0, SB_KV_HEADS, SB_HEAD_DIM), dt),
            jnp.zeros((Bp, 0, MLA_KV_RANK), dt), jnp.zeros((Bp, 0, MLA_ROPE), dt), mk, mv, *lw)
        p_lat.append(lat); p_kr.append(kr); p_k.append(kk); p_v.append(vv)
        p_h.append(hh); p_conv.append(cb); p_mk.append(mk); p_mv.append(mv)
        xs, lat, kr, kk, vv, hh, cb = trunk_layer(
            xs, state_lru_h[l], state_conv[l],
            gather_past(cache_sb_k, l, page_table), gather_past(cache_sb_v, l, page_table),
            gather_past(cache_mla_latent, l, page_table), gather_past(cache_mla_krope, l, page_table),
            cache_mem_k[l], cache_mem_v[l], *lw)
        s_lat.append(lat); s_kr.append(kr); s_k.append(kk); s_v.append(vv)
        s_h.append(hh); s_conv.append(cb)
    y_prompt = rmsnorm(xp, g_final)
    y_sample = rmsnorm(xs, g_final)
    return (y_prompt, y_sample,
            jnp.stack(p_lat), jnp.stack(p_kr), jnp.stack(p_k), jnp.stack(p_v),
            jnp.stack(p_h), jnp.stack(p_conv), jnp.stack(p_mk), jnp.stack(p_mv),
            jnp.stack(s_lat), jnp.stack(s_kr), jnp.stack(s_k), jnp.stack(s_v),
            jnp.stack(s_h), jnp.stack(s_conv))
```

```python
import functools

import numpy as np
import jax
import jax.numpy as jnp
from jax import lax
from jax.experimental import pallas as pl
from jax.experimental.pallas import tpu as pltpu

F32 = jnp.float32
BF16 = jnp.bfloat16

D_MODEL = 1024
PAGE = 128
LRU_W = 1024
LRU_BLOCKS = 16
CONV_W = 4
LRU_C = 8.0
SB_HEADS = 8
SB_KVH = 2
SB_G = 4
SB_D = 64
SB_SCALE = SB_D ** -0.5
MLA_H = 8
MLA_QR = 256
MLA_C = 128
MLA_NOPE = 64
MLA_ROPE = 32
MLA_V = 64
MLA_SCALE = (MLA_NOPE + MLA_ROPE) ** -0.5
ROPE_THETA = 10000.0
MEM_H = 4
MEM_D = 128
MEM_SCALE = MEM_D ** -0.5
D_FF = 4096
EPS = 1e-6

VMEM_LIMIT_V7X = 56 * 1024 * 1024
NEG_INF = float("-inf")


def _cparams(*sem):
    return pltpu.CompilerParams(dimension_semantics=sem, vmem_limit_bytes=VMEM_LIMIT_V7X)


def _rms(x, g):
    return x * lax.rsqrt(jnp.mean(x * x, axis=-1, keepdims=True) + EPS) * g


def _dot(a, b):
    return jnp.dot(a, b, preferred_element_type=F32)


def _dot_nt(a, b):
    return lax.dot_general(a, b, (((1,), (1,)), ((), ())), preferred_element_type=F32)


def _softplus(z):
    return jnp.maximum(z, 0.0) + jnp.log1p(jnp.exp(-jnp.abs(z)))


def _sigmoid(z):
    return 1.0 / (1.0 + jnp.exp(-z))


def _full(shape):
    n = len(shape)
    return pl.BlockSpec(shape, lambda *_: (0,) * n)


def _row_tile(n, want):
    t = min(want, n)
    while n % t:
        t //= 2
    return t


def _proj_kernel(x_ref, gmix_ref, wp_ref, gq_ref, wuq_ref, gkv_ref, bdk_ref, tabk_ref, tabq_ref,
                 qsb_ref, k_ref, v_ref, lat_ref, kr_ref, qabs_ref, qrope_ref):
    xn = _rms(x_ref[...], gmix_ref[...]).astype(BF16)
    u = _dot(xn, wp_ref[...])
    qsb_ref[...] = (u[:, 0:512] * SB_SCALE).astype(BF16)
    k_ref[...] = u[:, 512:640]
    v_ref[...] = u[:, 640:768]
    lat_ref[...] = _rms(u[:, 1024:1152], gkv_ref[...])
    tabk = tabk_ref[...]
    kr = u[:, 1152:1280] * tabk[:, :128] + u[:, 1280:1408] * tabk[:, 128:]
    kr_ref[...] = kr[:, :MLA_ROPE]
    cqn = _rms(u[:, 768:1024], gq_ref[...]).astype(BF16)
    qc = _dot(cqn, wuq_ref[...])
    qabs_ref[...] = _dot(qc[:, :512].astype(BF16), bdk_ref[...]).astype(BF16)
    tabq = tabq_ref[...]
    qrope_ref[...] = (qc[:, 512:768] * tabq[:, :256] + qc[:, 768:] * tabq[:, 256:]).astype(BF16)


def _proj(x, gmix, wp, gq, wuq, gkv, bdk, tabk, tabq):
    n = x.shape[0]
    tm = _row_tile(n, 512)
    row = lambda w: pl.BlockSpec((tm, w), lambda i: (i, 0))
    outs = [(512, BF16), (128, F32), (128, F32), (128, F32), (MLA_ROPE, F32), (1024, BF16), (256, BF16)]
    return pl.pallas_call(
        _proj_kernel,
        grid=(n // tm,),
        in_specs=[row(D_MODEL), _full(gmix.shape), _full(wp.shape), _full(gq.shape), _full(wuq.shape),
                  _full(gkv.shape), _full(bdk.shape), row(256), row(512)],
        out_specs=[row(w) for w, _ in outs],
        out_shape=[jax.ShapeDtypeStruct((n, w), dt) for w, dt in outs],
        compiler_params=_cparams("arbitrary"),
        name="proj",
    )(x, gmix, wp, gq, wuq, gkv, bdk, tabk, tabq)


def _lru_gates(xc, bda_ref, ba_ref, bdx_ref, bx_ref, lam_ref):
    xb = xc.astype(BF16)
    r = _sigmoid(_dot(xb, bda_ref[...]) + ba_ref[...])
    i = _sigmoid(_dot(xb, bdx_ref[...]) + bx_ref[...])
    log_a = -LRU_C * r * _softplus(-lam_ref[...])
    a = jnp.exp(log_a)
    mult = jnp.sqrt(jnp.maximum(1.0 - a * a, 0.0))
    return a, mult, i * xc


def _gelu_tanh(x):
    c = np.float32(np.sqrt(2.0 / np.pi))
    return 0.5 * x * (1.0 + jnp.tanh(c * (x + 0.044715 * (x * x * x))))


def _lru_prompt_kernel(x_ref, gmix_ref, w2_ref, cw_ref, cb_ref, bda_ref, ba_ref, bdx_ref, bx_ref, lam_ref,
                       g_ref, hlast_ref, conv_ref, ubuf, hc):
    t = pl.program_id(1)
    tc = x_ref.shape[1]

    @pl.when(t == 0)
    def _():
        ubuf[0:8, :] = jnp.zeros((8, LRU_W), F32)
        hc[...] = jnp.zeros_like(hc)

    xn = _rms(x_ref[0], gmix_ref[...]).astype(BF16)
    u = _dot(xn, w2_ref[...])
    ubuf[8:8 + tc, :] = u[:, :LRU_W]
    cw = cw_ref[...]
    xc = cb_ref[...]
    for k in range(CONV_W):
        xc = xc + ubuf[5 + k:5 + k + tc, :] * cw[k:k + 1, :]
    a, mult, ix = _lru_gates(xc, bda_ref, ba_ref, bdx_ref, bx_ref, lam_ref)
    rows = lax.broadcasted_iota(jnp.int32, (tc, LRU_W), 0)
    mult = jnp.where(rows == jnp.where(t == 0, 0, -1), 1.0, mult)
    b = mult * ix
    d = 1
    while d < tc:
        keep = rows >= d
        a_sh = jnp.where(keep, pltpu.roll(a, d, 0), 1.0)
        b_sh = jnp.where(keep, pltpu.roll(b, d, 0), 0.0)
        b = a * b_sh + b
        a = a * a_sh
        d *= 2
    h = a * hc[0:1, :] + b
    hc[0:1, :] = h[tc - 1:tc, :]
    hlast_ref[0] = h[tc - 1:tc, :]
    g_ref[0] = (h * _gelu_tanh(u[:, LRU_W:])).astype(BF16)
    tail = ubuf[tc:tc + 8, :]
    conv_ref[0] = tail
    ubuf[0:8, :] = tail


def _lru_prompt(xp, gmix, w2, cw, cb, bda, ba, bdx, bx, lam):
    b, t, _ = xp.shape
    tc = _row_tile(t, 256)
    assert tc >= 8
    ws = [gmix, w2, cw, cb, bda, ba, bdx, bx, lam]
    return pl.pallas_call(
        _lru_prompt_kernel,
        grid=(b, t // tc),
        in_specs=[pl.BlockSpec((1, tc, D_MODEL), lambda i, j: (i, j, 0))] + [_full(w.shape) for w in ws],
        out_specs=[pl.BlockSpec((1, tc, LRU_W), lambda i, j: (i, j, 0)),
                   pl.BlockSpec((1, 1, LRU_W), lambda i, j: (i, 0, 0)),
                   pl.BlockSpec((1, 8, LRU_W), lambda i, j: (i, 0, 0))],
        out_shape=[jax.ShapeDtypeStruct((b, t, LRU_W), BF16),
                   jax.ShapeDtypeStruct((b, 1, LRU_W), F32),
                   jax.ShapeDtypeStruct((b, 8, LRU_W), F32)],
        scratch_shapes=[pltpu.VMEM((tc + 8, LRU_W), F32), pltpu.VMEM((8, LRU_W), F32)],
        compiler_params=_cparams("arbitrary", "arbitrary"),
        name="lru_prompt",
    )(xp, *ws)


def _lru_sample_kernel(first_pos, x_ref, h0_ref, tail_ref, gmix_ref, w2_ref, cw_ref, cb_ref, bda_ref, ba_ref,
                       bdx_ref, bx_ref, lam_ref, g_ref, hlast_ref, conv_ref, ubuf):
    nb = h0_ref.shape[0]
    nt = x_ref.shape[0] // nb
    xn = _rms(x_ref[...], gmix_ref[...]).astype(BF16)
    u = _dot(xn, w2_ref[...])
    ubuf[0:(CONV_W - 1) * nb, :] = tail_ref[...]
    ubuf[(CONV_W - 1) * nb:, :] = u[:, :LRU_W]
    cw = cw_ref[...]
    xc = cb_ref[...]
    for k in range(CONV_W):
        xc = xc + ubuf[k * nb:(k + nt) * nb, :] * cw[k:k + 1, :]
    a, mult, ix = _lru_gates(xc, bda_ref, ba_ref, bdx_ref, bx_ref, lam_ref)
    gl = _gelu_tanh(u[:, LRU_W:])
    h = h0_ref[...]
    for t in range(nt):
        sl = slice(t * nb, (t + 1) * nb)
        m = mult[sl] if first_pos + t != 0 else jnp.ones_like(mult[sl])
        h = a[sl] * h + m * ix[sl]
        g_ref[sl, :] = (h * gl[sl]).astype(BF16)
    hlast_ref[...] = h
    conv_ref[...] = ubuf[nt * nb:(nt + CONV_W - 1) * nb, :]


def _lru_sample(first_pos, xs_tb, h0, tail, gmix, w2, cw, cb, bda, ba, bdx, bx, lam):
    n = xs_tb.shape[0]
    nb = h0.shape[0]
    args = [xs_tb, h0, tail, gmix, w2, cw, cb, bda, ba, bdx, bx, lam]
    return pl.pallas_call(
        functools.partial(_lru_sample_kernel, first_pos),
        grid=(1,),
        in_specs=[_full(a.shape) for a in args],
        out_specs=[_full((n, LRU_W)), _full((nb, LRU_W)), _full(((CONV_W - 1) * nb, LRU_W))],
        out_shape=[jax.ShapeDtypeStruct((n, LRU_W), BF16),
                   jax.ShapeDtypeStruct((nb, LRU_W), F32),
                   jax.ShapeDtypeStruct(((CONV_W - 1) * nb, LRU_W), F32)],
        scratch_shapes=[pltpu.VMEM((n + (CONV_W - 1) * nb, LRU_W), F32)],
        compiler_params=_cparams("arbitrary"),
        name="lru_sample",
    )(*args)


def _tri(n):
    j = lax.broadcasted_iota(jnp.int32, (n, n), 0)
    s = lax.broadcasted_iota(jnp.int32, (n, n), 1)
    return jnp.where(j >= s, 1.0, 0.0).astype(BF16)


def _suffix_sum(lk, tri):
    hi = lk.astype(BF16)
    lo = (lk - hi.astype(F32)).astype(BF16)
    return _dot(hi, tri) + _dot(lo, tri)


def _sb_block(z, mask, tri, carry):
    neg_sp = -_softplus(z)
    lk = neg_sp if mask is None else jnp.where(mask, neg_sp, 0.0)
    w = jnp.exp(z + _suffix_sum(lk, tri) + carry)
    if mask is not None:
        w = jnp.where(mask, w, 0.0)
    return w, carry + jnp.sum(lk, axis=-1, keepdims=True)


def _sb_prompt_kernel(q_ref, kt_ref, v_ref, o_ref, acc_ref, carry_ref):
    qi = pl.program_id(2)
    rows = q_ref.shape[3]
    tk = kt_ref.shape[4]
    tq = rows // SB_G
    q = q_ref[0, 0, 0]
    tri = _tri(tk)
    acc_ref[...] = jnp.zeros_like(acc_ref)
    carry_ref[...] = jnp.zeros_like(carry_ref)
    qpos = qi * tq + lax.rem(lax.broadcasted_iota(jnp.int32, (rows, tk), 0), tq)
    col = lax.broadcasted_iota(jnp.int32, (rows, tk), 1)

    def body(j, _):
        kb = qi - j
        z = _dot(q, kt_ref[0, 0, kb])
        mask = (kb * tk + col) < qpos
        w, carry = _sb_block(z, mask, tri, carry_ref[...])
        carry_ref[...] = carry
        acc_ref[...] += _dot(w.astype(BF16), v_ref[0, 0, kb])
        return 0

    lax.fori_loop(0, qi + 1, body, 0)
    o_ref[0, 0, 0] = acc_ref[...]


def _sb_prompt(q, kt, v):
    b, kvh, nq, rows, d = q.shape
    nk, tk = kt.shape[2], kt.shape[4]
    assert rows == SB_G * tk and nq == nk
    return pl.pallas_call(
        _sb_prompt_kernel,
        grid=(b, kvh, nq),
        in_specs=[pl.BlockSpec((1, 1, 1, rows, d), lambda i, h, j: (i, h, j, 0, 0)),
                  pl.BlockSpec((1, 1, nk, d, tk), lambda i, h, j: (i, h, 0, 0, 0)),
                  pl.BlockSpec((1, 1, nk, tk, d), lambda i, h, j: (i, h, 0, 0, 0))],
        out_specs=pl.BlockSpec((1, 1, 1, rows, d), lambda i, h, j: (i, h, j, 0, 0)),
        out_shape=jax.ShapeDtypeStruct(q.shape, F32),
        scratch_shapes=[pltpu.VMEM((rows, d), F32), pltpu.VMEM((rows, 1), F32)],
        compiler_params=_cparams("arbitrary", "arbitrary", "arbitrary"),
        name="sb_prompt",
    )(q, kt, v)


def _start_pages(pt_ref, b, n_pages, layer, pairs, slot):
    def body(p, _):
        pg = pt_ref[b, p]
        for cache, buf, sem in pairs:
            pltpu.make_async_copy(cache.at[layer, pg], buf.at[slot, p], sem.at[slot]).start()
        return 0
    lax.fori_loop(0, n_pages, body, 0)


def _wait_pages(n_pages, layer, pairs, slot):
    def body(p, _):
        for cache, buf, sem in pairs:
            pltpu.make_async_copy(cache.at[layer, 0], buf.at[slot, p], sem.at[slot]).wait()
        return 0
    lax.fori_loop(0, n_pages, body, 0)


def _paged_step(pt_ref, n_pages, layer, pairs):
    b = pl.program_id(0)
    slot = lax.rem(b, 2)

    @pl.when(b == 0)
    def _():
        _start_pages(pt_ref, b, n_pages, layer, pairs, slot)

    @pl.when(b + 1 < pl.num_programs(0))
    def _():
        _start_pages(pt_ref, b + 1, n_pages, layer, pairs, 1 - slot)

    _wait_pages(n_pages, layer, pairs, slot)
    return slot


def _sb_decode_kernel(layer, chunk, pt_ref, q_ref, kn_ref, vn_ref, ck_ref, cv_ref, o_ref,
                      kbuf, vbuf, ksem, vsem):
    n_pages = kbuf.shape[1]
    slot = _paged_step(pt_ref, n_pages, layer, [(ck_ref, kbuf, ksem), (cv_ref, vbuf, vsem)])
    q = q_ref[0]
    rows = q.shape[0]
    nt = rows // SB_HEADS
    tri = _tri(256)
    t_row = lax.rem(lax.broadcasted_iota(jnp.int32, (rows, PAGE), 0), nt)
    mask = lax.broadcasted_iota(jnp.int32, (rows, PAGE), 1) < t_row
    z = _dot_nt(q, kn_ref[0])
    w, carry = _sb_block(z, mask, tri[:PAGE, :PAGE], jnp.zeros((rows, 1), F32))
    acc = _dot(w.astype(BF16), vn_ref[0])
    ck = chunk * PAGE

    def body(j, st):
        acc, carry = st
        c = n_pages // chunk - 1 - j
        p0 = pl.multiple_of(c * chunk, chunk)
        k = kbuf[slot, pl.ds(p0, chunk)].reshape(ck, PAGE).astype(BF16)
        v = vbuf[slot, pl.ds(p0, chunk)].reshape(ck, PAGE).astype(BF16)
        z = _dot_nt(q, k)
        ws = [None] * (ck // 256)
        for s in reversed(range(ck // 256)):
            ws[s], carry = _sb_block(z[:, s * 256:(s + 1) * 256], None, tri, carry)
        w = jnp.concatenate(ws, axis=1) if len(ws) > 1 else ws[0]
        return acc + _dot(w.astype(BF16), v), carry

    acc, _ = lax.fori_loop(0, n_pages // chunk, body, (acc, carry))
    o_ref[0] = acc


def _sb_decode(layer, pt, q_bd, kn, vn, cache_k, cache_v):
    bs, rows, _ = q_bd.shape
    n_pages = pt.shape[1]
    chunk = min(8, n_pages)
    assert n_pages % chunk == 0 and (chunk * PAGE) % 256 == 0
    blk = lambda r, c: pl.BlockSpec((1, r, c), lambda i, pt: (i, 0, 0))
    return pl.pallas_call(
        functools.partial(_sb_decode_kernel, layer, chunk),
        grid_spec=pltpu.PrefetchScalarGridSpec(
            num_scalar_prefetch=1,
            grid=(bs,),
            in_specs=[blk(rows, PAGE), blk(PAGE, PAGE), blk(PAGE, PAGE),
                      pl.BlockSpec(memory_space=pl.ANY), pl.BlockSpec(memory_space=pl.ANY)],
            out_specs=blk(rows, PAGE),
            scratch_shapes=[pltpu.VMEM((2, n_pages, PAGE, PAGE), F32), pltpu.VMEM((2, n_pages, PAGE, PAGE), F32),
                            pltpu.SemaphoreType.DMA((2,)), pltpu.SemaphoreType.DMA((2,))]),
        out_shape=jax.ShapeDtypeStruct((bs, rows, PAGE), F32),
        compiler_params=_cparams("arbitrary"),
        name="sb_decode",
    )(pt, q_bd, kn, vn, cache_k, cache_v)


def _softmax_step(s, m, l, acc, v):
    m_new = jnp.maximum(m, jnp.max(s, axis=-1, keepdims=True))
    alpha = jnp.exp(m - m_new)
    p = jnp.exp(s - m_new)
    l = alpha * l + jnp.sum(p, axis=-1, keepdims=True)
    acc = alpha * acc + _dot(p.astype(BF16), v)
    return m_new, l, acc


def _mla_prompt_kernel(q_ref, kvt_ref, lat_ref, o_ref, acc_ref, m_ref, l_ref):
    qi = pl.program_id(1)
    rows = q_ref.shape[2]
    tk = kvt_ref.shape[3]
    tq = rows // MLA_H
    q = q_ref[0, 0]
    acc_ref[...] = jnp.zeros_like(acc_ref)
    m_ref[...] = jnp.full_like(m_ref, NEG_INF)
    l_ref[...] = jnp.zeros_like(l_ref)
    qpos = qi * tq + lax.rem(lax.broadcasted_iota(jnp.int32, (rows, tk), 0), tq)
    col = lax.broadcasted_iota(jnp.int32, (rows, tk), 1)

    def body(kb, _):
        s = _dot(q, kvt_ref[0, kb]) * MLA_SCALE
        s = jnp.where((kb * tk + col) <= qpos, s, NEG_INF)
        m, l, acc = _softmax_step(s, m_ref[...], l_ref[...], acc_ref[...], lat_ref[0, kb])
        m_ref[...] = m
        l_ref[...] = l
        acc_ref[...] = acc
        return 0

    lax.fori_loop(0, (qi * tq + tq + tk - 1) // tk, body, 0)
    o_ref[0, 0] = (acc_ref[...] / l_ref[...]).astype(BF16)


def _mla_prompt(q, kvt, lat):
    b, nq, rows, dq = q.shape
    nk, tk = kvt.shape[1], kvt.shape[3]
    return pl.pallas_call(
        _mla_prompt_kernel,
        grid=(b, nq),
        in_specs=[pl.BlockSpec((1, 1, rows, dq), lambda i, j: (i, j, 0, 0)),
                  pl.BlockSpec((1, nk, dq, tk), lambda i, j: (i, 0, 0, 0)),
                  pl.BlockSpec((1, nk, tk, MLA_C), lambda i, j: (i, 0, 0, 0))],
        out_specs=pl.BlockSpec((1, 1, rows, MLA_C), lambda i, j: (i, j, 0, 0)),
        out_shape=jax.ShapeDtypeStruct((b, nq, rows, MLA_C), BF16),
        scratch_shapes=[pltpu.VMEM((rows, MLA_C), F32), pltpu.VMEM((rows, 1), F32), pltpu.VMEM((rows, 1), F32)],
        compiler_params=_cparams("arbitrary", "arbitrary"),
        name="mla_prompt",
    )(q, kvt, lat)


def _mla_decode_kernel(layer, chunk, pt_ref, qa_ref, qr_ref, ln_ref, rn_ref, cl_ref, cr_ref, o_ref,
                       lbuf, rbuf, lsem, rsem):
    n_pages = lbuf.shape[1]
    slot = _paged_step(pt_ref, n_pages, layer, [(cl_ref, lbuf, lsem), (cr_ref, rbuf, rsem)])
    qa = qa_ref[0]
    qr = qr_ref[0]
    rows = qa.shape[0]
    nt = rows // MLA_H
    t_row = lax.rem(lax.broadcasted_iota(jnp.int32, (rows, PAGE), 0), nt)
    mask = lax.broadcasted_iota(jnp.int32, (rows, PAGE), 1) <= t_row
    ln = ln_ref[0]
    s = (_dot_nt(qa, ln) + _dot_nt(qr, rn_ref[0])) * MLA_SCALE
    s = jnp.where(mask, s, NEG_INF)
    st = _softmax_step(s, jnp.full((rows, 1), NEG_INF, F32), jnp.zeros((rows, 1), F32),
                       jnp.zeros((rows, MLA_C), F32), ln)
    ck = chunk * PAGE

    def body(c, st):
        p0 = pl.multiple_of(c * chunk, chunk)
        lat = lbuf[slot, pl.ds(p0, chunk)].reshape(ck, MLA_C).astype(BF16)
        kr = rbuf[slot, pl.ds(p0, chunk)].reshape(ck, MLA_ROPE).astype(BF16)
        s = (_dot_nt(qa, lat) + _dot_nt(qr, kr)) * MLA_SCALE
        return _softmax_step(s, *st, lat)

    m, l, acc = lax.fori_loop(0, n_pages // chunk, body, st)
    o_ref[0] = (acc / l).astype(BF16)


def _mla_decode(layer, pt, qa, qr, ln, rn, cache_l, cache_r):
    bs, rows, _ = qa.shape
    n_pages = pt.shape[1]
    chunk = min(8, n_pages)
    assert n_pages % chunk == 0
    blk = lambda r, c: pl.BlockSpec((1, r, c), lambda i, pt: (i, 0, 0))
    return pl.pallas_call(
        functools.partial(_mla_decode_kernel, layer, chunk),
        grid_spec=pltpu.PrefetchScalarGridSpec(
            num_scalar_prefetch=1,
            grid=(bs,),
            in_specs=[blk(rows, MLA_C), blk(rows, MLA_ROPE), blk(PAGE, MLA_C), blk(PAGE, MLA_ROPE),
                      pl.BlockSpec(memory_space=pl.ANY), pl.BlockSpec(memory_space=pl.ANY)],
            out_specs=blk(rows, MLA_C),
            scratch_shapes=[pltpu.VMEM((2, n_pages, PAGE, MLA_C), F32), pltpu.VMEM((2, n_pages, PAGE, MLA_ROPE), F32),
                            pltpu.SemaphoreType.DMA((2,)), pltpu.SemaphoreType.DMA((2,))]),
        out_shape=jax.ShapeDtypeStruct((bs, rows, MLA_C), BF16),
        compiler_params=_cparams("arbitrary"),
        name="mla_decode",
    )(pt, qa, qr, ln, rn, cache_l, cache_r)


def _merge_kernel(x_ref, g_ref, ob_ref, ol_ref, gmix_ref, wg_ref, bg_ref, wa_ref, wb_ref, bdv_ref, wc_ref,
                  wo_ref, gmem_ref, wmq_ref, x1_ref, qm_ref):
    x = x_ref[...]
    xn = _rms(x, gmix_ref[...]).astype(BF16)
    gate = _sigmoid(_dot(xn, wg_ref[...]) + bg_ref[...])
    ya = _dot(g_ref[...], wa_ref[...])
    yb = _dot(ob_ref[...], wb_ref[...])
    yc = _dot(_dot(ol_ref[...], bdv_ref[...]).astype(BF16), wc_ref[...])
    merged = gate[:, :D_MODEL] * ya + gate[:, D_MODEL:2 * D_MODEL] * yb + gate[:, 2 * D_MODEL:] * yc
    x1 = x + _dot(merged.astype(BF16), wo_ref[...])
    x1_ref[...] = x1
    qm_ref[...] = _dot(_rms(x1, gmem_ref[...]).astype(BF16), wmq_ref[...]).astype(BF16)


def _merge(x, g, ob, ol, gmix, wg, bg, wa, wb, bdv, wc, wo, gmem, wmq):
    n = x.shape[0]
    tm = _row_tile(n, 256)
    row = lambda w: pl.BlockSpec((tm, w), lambda i: (i, 0))
    ws = [gmix, wg, bg, wa, wb, bdv, wc, wo, gmem, wmq]
    return pl.pallas_call(
        _merge_kernel,
        grid=(n // tm,),
        in_specs=[row(D_MODEL), row(LRU_W), row(512), row(1024)] + [_full(w.shape) for w in ws],
        out_specs=[row(D_MODEL), row(512)],
        out_shape=[jax.ShapeDtypeStruct((n, D_MODEL), F32), jax.ShapeDtypeStruct((n, 512), BF16)],
        compiler_params=_cparams("arbitrary"),
        name="merge",
    )(x, g, ob, ol, *ws)


def _memkv_kernel(m_ref, g_ref, w_ref, o_ref):
    o_ref[0] = _dot(_rms(m_ref[...], g_ref[0]).astype(BF16), w_ref[0])


def _memkv(mem, g_memkv, w_kv):
    nl, _, wout = w_kv.shape
    rows = mem.shape[0]
    return pl.pallas_call(
        _memkv_kernel,
        grid=(nl,),
        in_specs=[_full(mem.shape), pl.BlockSpec((1, 1, D_MODEL), lambda l: (l, 0, 0)),
                  pl.BlockSpec((1, D_MODEL, wout), lambda l: (l, 0, 0))],
        out_specs=pl.BlockSpec((1, rows, wout), lambda l: (l, 0, 0)),
        out_shape=jax.ShapeDtypeStruct((nl, rows, wout), F32),
        compiler_params=_cparams("arbitrary"),
        name="memkv",
    )(mem, g_memkv, w_kv)


def _mem_heads(q, mk, mv):
    outs = []
    for h in range(MEM_H):
        sl = slice(h * MEM_D, (h + 1) * MEM_D)
        s = _dot_nt(q[:, sl], mk[:, sl]) * MEM_SCALE
        e = jnp.exp(s - jnp.max(s, axis=-1, keepdims=True))
        p = e / jnp.sum(e, axis=-1, keepdims=True)
        outs.append(_dot(p.astype(BF16), mv[:, sl]))
    return jnp.concatenate(outs, axis=1).astype(BF16)


def _memattn_kernel(q_ref, mk_ref, mv_ref, o_ref):
    o_ref[0] = _mem_heads(q_ref[0], mk_ref[0].astype(BF16), mv_ref[0].astype(BF16))


def _memattn(q, mk, mv, tq):
    b, t, w = q.shape
    ml = mk.shape[1]
    return pl.pallas_call(
        _memattn_kernel,
        grid=(b, t // tq),
        in_specs=[pl.BlockSpec((1, tq, w), lambda i, j: (i, j, 0)),
                  pl.BlockSpec((1, ml, w), lambda i, j: (i, 0, 0)),
                  pl.BlockSpec((1, ml, w), lambda i, j: (i, 0, 0))],
        out_specs=pl.BlockSpec((1, tq, w), lambda i, j: (i, j, 0)),
        out_shape=jax.ShapeDtypeStruct((b, t, w), BF16),
        compiler_params=_cparams("arbitrary", "arbitrary"),
        name="memattn",
    )(q, mk, mv)


def _mlp_kernel(final, x_ref, am_ref, wmo_ref, gmlp_ref, wup_ref, wdn_ref, gfin_ref, *rest):
    if final:
        o_ref, y_ref, x1_s, xn_s, acc_s = rest
    else:
        o_ref, x1_s, xn_s, acc_s = rest
    j = pl.program_id(1)

    @pl.when(j == 0)
    def _():
        x1 = x_ref[...] + _dot(am_ref[...], wmo_ref[...])
        x1_s[...] = x1
        xn_s[...] = _rms(x1, gmlp_ref[...]).astype(BF16)
        acc_s[...] = jnp.zeros_like(acc_s)

    h = jnp.maximum(_dot(xn_s[...], wup_ref[...]), 0.0)
    acc_s[...] += _dot((h * h).astype(BF16), wdn_ref[...])

    @pl.when(j == pl.num_programs(1) - 1)
    def _():
        out = x1_s[...] + acc_s[...]
        o_ref[...] = out
        if final:
            y_ref[...] = _rms(out, gfin_ref[...])


def _mlp(x, am, wmo, gmlp, wup, wdn, gfin, final):
    n = x.shape[0]
    tm = _row_tile(n, 512)
    tf = 1024
    row = lambda w: pl.BlockSpec((tm, w), lambda i, j: (i, 0))
    cst = lambda s: pl.BlockSpec(s, lambda i, j: (0, 0))
    n_out = 2 if final else 1
    outs = pl.pallas_call(
        functools.partial(_mlp_kernel, final),
        grid=(n // tm, D_FF // tf),
        in_specs=[row(D_MODEL), row(512), cst(wmo.shape), cst(gmlp.shape),
                  pl.BlockSpec((D_MODEL, tf), lambda i, j: (0, j)),
                  pl.BlockSpec((tf, D_MODEL), lambda i, j: (j, 0)), cst(gfin.shape)],
        out_specs=[row(D_MODEL)] * n_out,
        out_shape=[jax.ShapeDtypeStruct((n, D_MODEL), F32)] * n_out,
        scratch_shapes=[pltpu.VMEM((tm, D_MODEL), F32), pltpu.VMEM((tm, D_MODEL), BF16),
                        pltpu.VMEM((tm, D_MODEL), F32)],
        compiler_params=_cparams("arbitrary", "arbitrary"),
        name="mlp",
    )(x, am, wmo, gmlp, wup, wdn, gfin)
    return outs


def _rope_tables(pos):
    half = MLA_ROPE // 2
    inv = ROPE_THETA ** (-jnp.arange(half, dtype=F32) / half)
    ang = pos.astype(F32)[:, None] * inv[None, :]
    cos, sin = jnp.cos(ang), jnp.sin(ang)
    c = jnp.concatenate([cos, cos], axis=1)
    s = jnp.concatenate([-sin, sin], axis=1)
    pad = jnp.zeros((pos.shape[0], 128 - MLA_ROPE), F32)
    tabk = jnp.concatenate([c, pad, s, pad], axis=1)
    tabq = jnp.concatenate([jnp.tile(c, (1, MLA_H)), jnp.tile(s, (1, MLA_H))], axis=1)
    return tabk, tabq


def _swap_halves(w):
    half = w.shape[-1] // 2
    return jnp.concatenate([w[..., half:], w[..., :half]], axis=-1)


def _block_diag(w):
    n, a, b = w.shape
    return jnp.einsum('nab,nm->namb', w, jnp.eye(n, dtype=w.dtype)).reshape(n * a, n * b)


def kernel(x_prompt, x_sample, mem_prompt, cache_mla_latent, cache_mla_krope, cache_sb_k, cache_sb_v, cache_mem_k, cache_mem_v, state_lru_h, state_conv, page_table, g_mix, w_in, b_gate, conv_w, conv_b, lru_wa, lru_ba, lru_wx, lru_bx, lru_lambda, mla_gq, mla_wuq, mla_gkv, mla_wuk, mla_wuv, w_lru_o, w_sb_o, w_mla_o, w_out, g_mem, g_memkv, w_mq, w_mk, w_mv, w_mo, g_mlp, w_up, w_down, g_final):
    bp, tp, _ = x_prompt.shape
    bs, ts, _ = x_sample.shape
    depth = w_in.shape[0]
    n_pool = cache_sb_k.shape[1]
    n_pages = page_table.shape[1]
    past = n_pages * PAGE
    mem_len = mem_prompt.shape[1]
    n_p, n_s = bp * tp, bs * ts
    assert ts * SB_HEADS == 32 and ts * MLA_H == 32

    pos = jnp.concatenate([jnp.tile(jnp.arange(tp, dtype=jnp.int32), bp),
                           jnp.tile(past + jnp.arange(ts, dtype=jnp.int32), bs)])
    tabk, tabq = _rope_tables(pos)

    tq_sb = _row_tile(tp, 256)
    nq_sb = tp // tq_sb
    tq_mla = _row_tile(tp, 128)
    tk_mla = _row_tile(tp, 256)
    nq_mla, nk_mla = tp // tq_mla, tp // tk_mla

    ck = cache_sb_k.reshape(depth, n_pool, PAGE, SB_KVH * SB_D)
    cv = cache_sb_v.reshape(depth, n_pool, PAGE, SB_KVH * SB_D)
    cmk = cache_mem_k.reshape(depth, bs, mem_len, MEM_H * MEM_D)
    cmv = cache_mem_v.reshape(depth, bs, mem_len, MEM_H * MEM_D)

    mem_kv = _memkv(mem_prompt.reshape(bp * mem_len, D_MODEL), g_memkv[:, None, :],
                    jnp.concatenate([w_mk, w_mv], axis=2).astype(BF16))
    p_mk = mem_kv[:, :, :MEM_H * MEM_D].reshape(depth, bp, mem_len, MEM_H * MEM_D)
    p_mv = mem_kv[:, :, MEM_H * MEM_D:].reshape(depth, bp, mem_len, MEM_H * MEM_D)

    x = jnp.concatenate([x_prompt.reshape(n_p, D_MODEL), x_sample.reshape(n_s, D_MODEL)], axis=0)
    outs = {k: [] for k in ("p_lat", "p_kr", "p_k", "p_v", "p_h", "p_conv",
                            "s_lat", "s_kr", "s_k", "s_v", "s_h", "s_conv")}
    row2 = lambda v: v.reshape(1, -1)
    zpad = lambda a, n: jnp.pad(a, ((0, 0), (0, n - a.shape[1]), (0, 0)))
    y = None
    for l in range(depth):
        wl = w_in[l]
        w2 = wl[:, :2 * LRU_W].astype(BF16)
        kr_w = wl[:, 3200:3232]
        zc = jnp.zeros((D_MODEL, 128 - MLA_ROPE), F32)
        wp = jnp.concatenate([wl[:, 2048:3200], kr_w, zc, _swap_halves(kr_w), zc], axis=1).astype(BF16)
        wg = wl[:, 3232:].astype(BF16)
        wuq3 = mla_wuq[l].reshape(MLA_QR, MLA_H, MLA_NOPE + MLA_ROPE)
        rope_w = wuq3[:, :, MLA_NOPE:]
        wuq = jnp.concatenate([wuq3[:, :, :MLA_NOPE].reshape(MLA_QR, -1), rope_w.reshape(MLA_QR, -1),
                               _swap_halves(rope_w).reshape(MLA_QR, -1)], axis=1).astype(BF16)
        bdk = _block_diag(jnp.transpose(mla_wuk[l], (1, 2, 0))).astype(BF16)
        bdv = _block_diag(jnp.transpose(mla_wuv[l], (1, 0, 2))).astype(BF16)
        bda = _block_diag(lru_wa[l]).astype(BF16)
        bdx = _block_diag(lru_wx[l]).astype(BF16)
        gmix = row2(g_mix[l])
        lru_w = [gmix, w2, conv_w[l], row2(conv_b[l]), bda, row2(lru_ba[l]), bdx, row2(lru_bx[l]),
                 row2(lru_lambda[l])]

        qsb, k_new, v_new, lat_new, kr_new, qabs, qrope = _proj(
            x, gmix, wp, row2(mla_gq[l]), wuq, row2(mla_gkv[l]), bdk, tabk, tabq)

        g_p, h_p, conv_p = _lru_prompt(x[:n_p].reshape(bp, tp, D_MODEL), *lru_w)
        xs_tb = x[n_p:].reshape(bs, ts, D_MODEL).transpose(1, 0, 2).reshape(n_s, D_MODEL)
        tail = state_conv[l].transpose(1, 0, 2).reshape((CONV_W - 1) * bs, LRU_W)
        g_s, h_s, conv_s = _lru_sample(past, xs_tb, state_lru_h[l], tail, *lru_w)
        g_s = g_s.reshape(ts, bs, LRU_W).transpose(1, 0, 2).reshape(n_s, LRU_W)
        g_all = jnp.concatenate([g_p.reshape(n_p, LRU_W), g_s], axis=0)
        outs["p_h"].append(h_p.reshape(bp, LRU_W))
        outs["p_conv"].append(conv_p[:, 8 - (CONV_W - 1):, :])
        outs["s_h"].append(h_s)
        outs["s_conv"].append(conv_s.reshape(CONV_W - 1, bs, LRU_W).transpose(1, 0, 2))

        kb, vb = k_new.astype(BF16), v_new.astype(BF16)
        q5 = qsb[:n_p].reshape(bp, nq_sb, tq_sb, SB_KVH, SB_G, SB_D).transpose(0, 3, 1, 4, 2, 5)
        q5 = q5.reshape(bp, SB_KVH, nq_sb, SB_G * tq_sb, SB_D)
        k5 = kb[:n_p].reshape(bp, nq_sb, tq_sb, SB_KVH, SB_D)
        v5 = vb[:n_p].reshape(bp, nq_sb, tq_sb, SB_KVH, SB_D)
        ob_p = _sb_prompt(q5, k5.transpose(0, 3, 1, 4, 2), v5.transpose(0, 3, 1, 2, 4))
        ob_p = ob_p.reshape(bp, SB_KVH, nq_sb, SB_G, tq_sb, SB_D).transpose(0, 2, 4, 1, 3, 5)
        ob_p = ob_p.reshape(n_p, SB_HEADS * SB_D).astype(BF16)

        qs = qsb[n_p:].reshape(bs, ts, SB_KVH, SB_G, SB_D).transpose(0, 2, 3, 1, 4)
        q_bd = jnp.einsum('bhgtd,hk->bhgtkd', qs, jnp.eye(SB_KVH, dtype=BF16))
        q_bd = q_bd.reshape(bs, SB_HEADS * ts, SB_KVH * SB_D)
        kn = zpad(kb[n_p:].reshape(bs, ts, SB_KVH * SB_D), PAGE)
        vn = zpad(vb[n_p:].reshape(bs, ts, SB_KVH * SB_D), PAGE)
        ob_s = _sb_decode(l, page_table, q_bd, kn, vn, ck, cv)
        ob_s = ob_s.reshape(bs, SB_KVH, SB_G, ts, SB_KVH, SB_D)
        ob_s = jnp.stack([ob_s[:, h, :, :, h, :] for h in range(SB_KVH)], axis=1)
        ob_s = ob_s.transpose(0, 3, 1, 2, 4).reshape(n_s, SB_HEADS * SB_D).astype(BF16)
        ob = jnp.concatenate([ob_p, ob_s], axis=0)

        latb, krb = lat_new.astype(BF16), kr_new.astype(BF16)
        qa4 = qabs[:n_p].reshape(bp, nq_mla, tq_mla, MLA_H, MLA_C).transpose(0, 1, 3, 2, 4)
        qr4 = qrope[:n_p].reshape(bp, nq_mla, tq_mla, MLA_H, MLA_ROPE).transpose(0, 1, 3, 2, 4)
        qcat = jnp.concatenate([qa4, qr4, jnp.zeros(qa4.shape[:-1] + (256 - MLA_C - MLA_ROPE,), BF16)], axis=-1)
        qcat = qcat.reshape(bp, nq_mla, MLA_H * tq_mla, 256)
        lat4 = latb[:n_p].reshape(bp, nk_mla, tk_mla, MLA_C)
        kv4 = jnp.concatenate([lat4, krb[:n_p].reshape(bp, nk_mla, tk_mla, MLA_ROPE),
                               jnp.zeros((bp, nk_mla, tk_mla, 256 - MLA_C - MLA_ROPE), BF16)], axis=-1)
        ol_p = _mla_prompt(qcat, kv4.transpose(0, 1, 3, 2), lat4)
        ol_p = ol_p.reshape(bp, nq_mla, MLA_H, tq_mla, MLA_C).transpose(0, 1, 3, 2, 4).reshape(n_p, MLA_H * MLA_C)

        qa_s = qabs[n_p:].reshape(bs, ts, MLA_H, MLA_C).transpose(0, 2, 1, 3).reshape(bs, MLA_H * ts, MLA_C)
        qr_s = qrope[n_p:].reshape(bs, ts, MLA_H, MLA_ROPE).transpose(0, 2, 1, 3).reshape(bs, MLA_H * ts, MLA_ROPE)
        ln = zpad(latb[n_p:].reshape(bs, ts, MLA_C), PAGE)
        rn = zpad(krb[n_p:].reshape(bs, ts, MLA_ROPE), PAGE)
        ol_s = _mla_decode(l, page_table, qa_s, qr_s, ln, rn, cache_mla_latent, cache_mla_krope)
        ol_s = ol_s.reshape(bs, MLA_H, ts, MLA_C).transpose(0, 2, 1, 3).reshape(n_s, MLA_H * MLA_C)
        ol = jnp.concatenate([ol_p, ol_s], axis=0)

        outs["p_lat"].append(lat_new[:n_p].reshape(bp, tp, MLA_C))
        outs["p_kr"].append(kr_new[:n_p].reshape(bp, tp, MLA_ROPE))
        outs["p_k"].append(k_new[:n_p].reshape(bp, tp, SB_KVH, SB_D))
        outs["p_v"].append(v_new[:n_p].reshape(bp, tp, SB_KVH, SB_D))
        outs["s_lat"].append(lat_new[n_p:].reshape(bs, ts, MLA_C))
        outs["s_kr"].append(kr_new[n_p:].reshape(bs, ts, MLA_ROPE))
        outs["s_k"].append(k_new[n_p:].reshape(bs, ts, SB_KVH, SB_D))
        outs["s_v"].append(v_new[n_p:].reshape(bs, ts, SB_KVH, SB_D))

        x1, qm = _merge(x, g_all, ob, ol, gmix, wg, row2(b_gate[l]), w_lru_o[l].astype(BF16),
                        w_sb_o[l].astype(BF16), bdv, w_mla_o[l].astype(BF16), w_out[l].astype(BF16),
                        row2(g_mem[l]), w_mq[l].astype(BF16))
        am_p = _memattn(qm[:n_p].reshape(bp, tp, MEM_H * MEM_D), p_mk[l], p_mv[l], _row_tile(tp, 512))
        am_s = _memattn(qm[n_p:].reshape(bs, ts, MEM_H * MEM_D), cmk[l], cmv[l], ts)
        am = jnp.concatenate([am_p.reshape(n_p, -1), am_s.reshape(n_s, -1)], axis=0)
        final = l == depth - 1
        res = _mlp(x1, am, w_mo[l].astype(BF16), row2(g_mlp[l]), w_up[l].astype(BF16), w_down[l].astype(BF16),
                   row2(g_final), final)
        x = res[0]
        if final:
            y = res[1]

    st = {k: jnp.stack(v) for k, v in outs.items()}
    mshape = (depth, bp, mem_len, MEM_H, MEM_D)
    return (y[:n_p].reshape(bp, tp, D_MODEL), y[n_p:].reshape(bs, ts, D_MODEL),
            st["p_lat"], st["p_kr"], st["p_k"], st["p_v"], st["p_h"], st["p_conv"],
            p_mk.reshape(mshape), p_mv.reshape(mshape),
            st["s_lat"], st["s_kr"], st["s_k"], st["s_v"], st["s_h"], st["s_conv"])
```

```python
import functools

import numpy as np
import jax
import jax.numpy as jnp
from jax import lax
from jax.experimental import pallas as pl
from jax.experimental.pallas import tpu as pltpu

F32 = jnp.float32
BF16 = jnp.bfloat16

D_MODEL = 1024
PAGE = 128
LRU_W = 1024
LRU_BLOCKS = 16
CONV_W = 4
LRU_C = 8.0
SB_HEADS = 8
SB_KVH = 2
SB_G = 4
SB_D = 64
SB_SCALE = SB_D ** -0.5
MLA_H = 8
MLA_QR = 256
MLA_C = 128
MLA_NOPE = 64
MLA_ROPE = 32
MLA_V = 64
MLA_SCALE = (MLA_NOPE + MLA_ROPE) ** -0.5
ROPE_THETA = 10000.0
MEM_H = 4
MEM_D = 128
MEM_SCALE = MEM_D ** -0.5
D_FF = 4096
EPS = 1e-6

SB_TQ = 256
MLA_TQ = 128
MLA_TK = 256
SB_SUB = 1024
MLA_SUB = 512
LOG2E = 1.4426950408889634

VMEM_LIMIT_V7X = 56 * 1024 * 1024
NEG_INF = float("-inf")


def _cparams(*sem):
    return pltpu.CompilerParams(dimension_semantics=sem, vmem_limit_bytes=VMEM_LIMIT_V7X)


def _rms(x, g):
    return x * lax.rsqrt(jnp.mean(x * x, axis=-1, keepdims=True) + EPS) * g


def _dot(a, b):
    return jnp.dot(a, b, preferred_element_type=F32)


def _dot_nt(a, b):
    return lax.dot_general(a, b, (((1,), (1,)), ((), ())), preferred_element_type=F32)


def _softplus(z):
    return jnp.maximum(z, 0.0) + jnp.log1p(jnp.exp(-jnp.abs(z)))


def _softplus_fast(z):
    return jnp.maximum(z, 0.0) + jnp.log(1.0 + jnp.exp(-jnp.abs(z)))


def _sigmoid(z):
    return 1.0 / (1.0 + jnp.exp(-z))


def _full(shape):
    n = len(shape)
    return pl.BlockSpec(shape, lambda *_: (0,) * n)


def _row_tile(n, want):
    t = min(want, n)
    while n % t:
        t //= 2
    return t


def _proj_kernel(x_ref, gmix_ref, wp_ref, gq_ref, wuq_ref, gkv_ref, bdk_ref, tabk_ref, tabq_ref,
                 qsb_ref, k_ref, v_ref, lat_ref, kr_ref, qabs_ref, qrope_ref):
    xn = _rms(x_ref[...], gmix_ref[...]).astype(BF16)
    u = _dot(xn, wp_ref[...])
    qsb_ref[...] = (u[:, 0:512] * SB_SCALE).astype(BF16)
    k_ref[...] = u[:, 512:640]
    v_ref[...] = u[:, 640:768]
    lat_ref[...] = _rms(u[:, 1024:1152], gkv_ref[...])
    tabk = tabk_ref[...]
    kr = u[:, 1152:1280] * tabk[:, :128] + u[:, 1280:1408] * tabk[:, 128:]
    kr_ref[...] = kr[:, :MLA_ROPE]
    cqn = _rms(u[:, 768:1024], gq_ref[...]).astype(BF16)
    qc = _dot(cqn, wuq_ref[...])
    qabs_ref[...] = _dot(qc[:, :512].astype(BF16), bdk_ref[...]).astype(BF16)
    tabq = tabq_ref[...]
    qrope_ref[...] = (qc[:, 512:768] * tabq[:, :256] + qc[:, 768:] * tabq[:, 256:]).astype(BF16)


def _proj(x, gmix, wp, gq, wuq, gkv, bdk, tabk, tabq):
    n = x.shape[0]
    tm = _row_tile(n, 512)
    row = lambda w: pl.BlockSpec((tm, w), lambda i: (i, 0))
    outs = [(512, BF16), (128, F32), (128, F32), (128, F32), (MLA_ROPE, F32), (1024, BF16), (256, BF16)]
    return pl.pallas_call(
        _proj_kernel,
        grid=(n // tm,),
        in_specs=[row(D_MODEL), _full(gmix.shape), _full(wp.shape), _full(gq.shape), _full(wuq.shape),
                  _full(gkv.shape), _full(bdk.shape), row(256), row(512)],
        out_specs=[row(w) for w, _ in outs],
        out_shape=[jax.ShapeDtypeStruct((n, w), dt) for w, dt in outs],
        compiler_params=_cparams("arbitrary"),
        name="proj",
    )(x, gmix, wp, gq, wuq, gkv, bdk, tabk, tabq)


def _lru_gates(xc, bda_ref, ba_ref, bdx_ref, bx_ref, lam_ref):
    xb = xc.astype(BF16)
    r = _sigmoid(_dot(xb, bda_ref[...]) + ba_ref[...])
    i = _sigmoid(_dot(xb, bdx_ref[...]) + bx_ref[...])
    log_a = -LRU_C * r * _softplus(-lam_ref[...])
    a = jnp.exp(log_a)
    mult = jnp.sqrt(jnp.maximum(1.0 - a * a, 0.0))
    return a, mult, i * xc


def _gelu_tanh(x):
    c = np.float32(np.sqrt(2.0 / np.pi))
    return 0.5 * x * (1.0 + jnp.tanh(c * (x + 0.044715 * (x * x * x))))


def _lru_prompt_kernel(x_ref, gmix_ref, w2_ref, cw_ref, cb_ref, bda_ref, ba_ref, bdx_ref, bx_ref, lam_ref,
                       g_ref, hlast_ref, conv_ref, ubuf, hc):
    t = pl.program_id(1)
    tc = x_ref.shape[1]

    @pl.when(t == 0)
    def _():
        ubuf[0:8, :] = jnp.zeros((8, LRU_W), F32)
        hc[...] = jnp.zeros_like(hc)

    xn = _rms(x_ref[0], gmix_ref[...]).astype(BF16)
    u = _dot(xn, w2_ref[...])
    ubuf[8:8 + tc, :] = u[:, :LRU_W]
    cw = cw_ref[...]
    xc = cb_ref[...]
    for k in range(CONV_W):
        xc = xc + ubuf[5 + k:5 + k + tc, :] * cw[k:k + 1, :]
    a, mult, ix = _lru_gates(xc, bda_ref, ba_ref, bdx_ref, bx_ref, lam_ref)
    rows = lax.broadcasted_iota(jnp.int32, (tc, LRU_W), 0)
    mult = jnp.where(rows == jnp.where(t == 0, 0, -1), 1.0, mult)
    b = mult * ix
    d = 1
    while d < tc:
        keep = rows >= d
        a_sh = jnp.where(keep, pltpu.roll(a, d, 0), 1.0)
        b_sh = jnp.where(keep, pltpu.roll(b, d, 0), 0.0)
        b = a * b_sh + b
        a = a * a_sh
        d *= 2
    h = a * hc[0:1, :] + b
    hc[0:1, :] = h[tc - 1:tc, :]
    hlast_ref[0] = h[tc - 1:tc, :]
    g_ref[0] = (h * _gelu_tanh(u[:, LRU_W:])).astype(BF16)
    tail = ubuf[tc:tc + 8, :]
    conv_ref[0] = tail
    ubuf[0:8, :] = tail


def _lru_prompt(xp, gmix, w2, cw, cb, bda, ba, bdx, bx, lam):
    b, t, _ = xp.shape
    tc = _row_tile(t, 256)
    assert tc >= 8
    ws = [gmix, w2, cw, cb, bda, ba, bdx, bx, lam]
    return pl.pallas_call(
        _lru_prompt_kernel,
        grid=(b, t // tc),
        in_specs=[pl.BlockSpec((1, tc, D_MODEL), lambda i, j: (i, j, 0))] + [_full(w.shape) for w in ws],
        out_specs=[pl.BlockSpec((1, tc, LRU_W), lambda i, j: (i, j, 0)),
                   pl.BlockSpec((1, 1, LRU_W), lambda i, j: (i, 0, 0)),
                   pl.BlockSpec((1, 8, LRU_W), lambda i, j: (i, 0, 0))],
        out_shape=[jax.ShapeDtypeStruct((b, t, LRU_W), BF16),
                   jax.ShapeDtypeStruct((b, 1, LRU_W), F32),
                   jax.ShapeDtypeStruct((b, 8, LRU_W), F32)],
        scratch_shapes=[pltpu.VMEM((tc + 8, LRU_W), F32), pltpu.VMEM((8, LRU_W), F32)],
        compiler_params=_cparams("arbitrary", "arbitrary"),
        name="lru_prompt",
    )(xp, *ws)


def _lru_sample_kernel(first_pos, x_ref, h0_ref, tail_ref, gmix_ref, w2_ref, cw_ref, cb_ref, bda_ref, ba_ref,
                       bdx_ref, bx_ref, lam_ref, g_ref, hlast_ref, conv_ref, ubuf):
    nb = h0_ref.shape[0]
    nt = x_ref.shape[0] // nb
    xn = _rms(x_ref[...], gmix_ref[...]).astype(BF16)
    u = _dot(xn, w2_ref[...])
    ubuf[0:(CONV_W - 1) * nb, :] = tail_ref[...]
    ubuf[(CONV_W - 1) * nb:, :] = u[:, :LRU_W]
    cw = cw_ref[...]
    xc = cb_ref[...]
    for k in range(CONV_W):
        xc = xc + ubuf[k * nb:(k + nt) * nb, :] * cw[k:k + 1, :]
    a, mult, ix = _lru_gates(xc, bda_ref, ba_ref, bdx_ref, bx_ref, lam_ref)
    gl = _gelu_tanh(u[:, LRU_W:])
    h = h0_ref[...]
    for t in range(nt):
        sl = slice(t * nb, (t + 1) * nb)
        m = mult[sl] if first_pos + t != 0 else jnp.ones_like(mult[sl])
        h = a[sl] * h + m * ix[sl]
        g_ref[sl, :] = (h * gl[sl]).astype(BF16)
    hlast_ref[...] = h
    conv_ref[...] = ubuf[nt * nb:(nt + CONV_W - 1) * nb, :]


def _lru_sample(first_pos, xs_tb, h0, tail, gmix, w2, cw, cb, bda, ba, bdx, bx, lam):
    n = xs_tb.shape[0]
    nb = h0.shape[0]
    args = [xs_tb, h0, tail, gmix, w2, cw, cb, bda, ba, bdx, bx, lam]
    return pl.pallas_call(
        functools.partial(_lru_sample_kernel, first_pos),
        grid=(1,),
        in_specs=[_full(a.shape) for a in args],
        out_specs=[_full((n, LRU_W)), _full((nb, LRU_W)), _full(((CONV_W - 1) * nb, LRU_W))],
        out_shape=[jax.ShapeDtypeStruct((n, LRU_W), BF16),
                   jax.ShapeDtypeStruct((nb, LRU_W), F32),
                   jax.ShapeDtypeStruct(((CONV_W - 1) * nb, LRU_W), F32)],
        scratch_shapes=[pltpu.VMEM((n + (CONV_W - 1) * nb, LRU_W), F32)],
        compiler_params=_cparams("arbitrary"),
        name="lru_sample",
    )(*args)


def _tri2(n):
    j = lax.rem(lax.broadcasted_iota(jnp.int32, (2 * n, n), 0), n)
    s = lax.broadcasted_iota(jnp.int32, (2 * n, n), 1)
    return jnp.where(j >= s, 1.0, 0.0).astype(BF16)


def _hi_lo(x):
    hi = x.astype(BF16)
    lo = (x - hi.astype(F32)).astype(BF16)
    return jnp.concatenate([hi, lo], axis=1)


def _lanes(x, n):
    return x if n == 128 else jnp.concatenate([x] * (n // 128), axis=1)


def _sb_weights(z, mask, tri2, carry):
    sp = _softplus_fast(z)
    if mask is not None:
        sp = jnp.where(mask, sp, 0.0)
    w = jnp.exp(z - _dot(_hi_lo(sp), tri2) - _lanes(carry, z.shape[1]))
    if mask is not None:
        w = jnp.where(mask, w, 0.0)
    return w, carry + jnp.sum(sp, axis=-1, keepdims=True)


def _sb_prompt_kernel(q_ref, kt_ref, v_ref, o_ref, acc_ref, carry_ref):
    qi = pl.program_id(2)
    rows = q_ref.shape[3]
    tk = kt_ref.shape[4]
    tq = rows // SB_G
    sub = min(SB_SUB, rows)
    tri2 = _tri2(tk)
    acc_ref[...] = jnp.zeros_like(acc_ref)
    carry_ref[...] = jnp.zeros_like(carry_ref)
    rowi = lax.broadcasted_iota(jnp.int32, (sub, tk), 0)
    col = lax.broadcasted_iota(jnp.int32, (sub, tk), 1)

    def block(kb, diag):
        kt = kt_ref[0, 0, kb]
        v = v_ref[0, 0, kb]
        for sb in range(rows // sub):
            r = slice(sb * sub, (sb + 1) * sub)
            z = _dot(q_ref[0, 0, 0, r, :], kt)
            mask = col < lax.rem(rowi + sb * sub, tq) if diag else None
            w, carry = _sb_weights(z, mask, tri2, carry_ref[r, :])
            carry_ref[r, :] = carry
            acc_ref[r, :] += _dot(w.astype(BF16), v)

    block(qi, True)

    def body(j, _):
        block(qi - 1 - j, False)
        return 0

    lax.fori_loop(0, qi, body, 0)
    o_ref[0, 0, 0] = acc_ref[...]


def _sb_prompt(q, kt, v):
    b, kvh, nq, rows, d = q.shape
    nk, tk = kt.shape[2], kt.shape[4]
    assert rows == SB_G * tk and nq == nk
    return pl.pallas_call(
        _sb_prompt_kernel,
        grid=(b, kvh, nq),
        in_specs=[pl.BlockSpec((1, 1, 1, rows, d), lambda i, h, j: (i, h, j, 0, 0)),
                  pl.BlockSpec((1, 1, nk, d, tk), lambda i, h, j: (i, h, 0, 0, 0)),
                  pl.BlockSpec((1, 1, nk, tk, d), lambda i, h, j: (i, h, 0, 0, 0))],
        out_specs=pl.BlockSpec((1, 1, 1, rows, d), lambda i, h, j: (i, h, j, 0, 0)),
        out_shape=jax.ShapeDtypeStruct(q.shape, F32),
        scratch_shapes=[pltpu.VMEM((rows, d), F32), pltpu.VMEM((rows, 128), F32)],
        compiler_params=_cparams("arbitrary", "arbitrary", "arbitrary"),
        name="sb_prompt",
    )(q, kt, v)


def _start_pages(pt_ref, b, n_pages, layer, pairs, slot):
    def body(p, _):
        pg = pt_ref[b, p]
        for cache, buf, sem in pairs:
            pltpu.make_async_copy(cache.at[layer, pg], buf.at[slot, p], sem.at[slot]).start()
        return 0
    lax.fori_loop(0, n_pages, body, 0)


def _wait_pages(n_pages, layer, pairs, slot):
    def body(p, _):
        for cache, buf, sem in pairs:
            pltpu.make_async_copy(cache.at[layer, 0], buf.at[slot, p], sem.at[slot]).wait()
        return 0
    lax.fori_loop(0, n_pages, body, 0)


def _paged_step(pt_ref, n_pages, layer, pairs):
    b = pl.program_id(0)
    slot = lax.rem(b, 2)

    @pl.when(b == 0)
    def _():
        _start_pages(pt_ref, b, n_pages, layer, pairs, slot)

    @pl.when(b + 1 < pl.num_programs(0))
    def _():
        _start_pages(pt_ref, b + 1, n_pages, layer, pairs, 1 - slot)

    _wait_pages(n_pages, layer, pairs, slot)
    return slot


def _sb_decode_kernel(layer, pt_ref, q_ref, kn_ref, vn_ref, ck_ref, cv_ref, o_ref,
                      kbuf, vbuf, ksem, vsem):
    n_pages = kbuf.shape[1]
    slot = _paged_step(pt_ref, n_pages, layer, [(ck_ref, kbuf, ksem), (cv_ref, vbuf, vsem)])
    q = q_ref[0]
    rows = q.shape[0]
    nt = rows // SB_HEADS
    t_row = lax.rem(lax.broadcasted_iota(jnp.int32, (rows, PAGE), 0), nt)
    mask = lax.broadcasted_iota(jnp.int32, (rows, PAGE), 1) < t_row
    w, carry = _sb_weights(_dot(q, kn_ref[0]), mask, _tri2(PAGE), jnp.zeros((rows, 128), F32))
    acc = _dot_nt(w.astype(BF16), vn_ref[0])
    nb = n_pages // 2
    zs = [jnp.concatenate([_dot(q, kbuf[slot, 2 * b + i].astype(BF16)) for i in range(2)], axis=1)
          for b in range(nb)]
    sps = [_softplus_fast(z) for z in zs]
    suf = _dot(jnp.concatenate([_hi_lo(sp) for sp in sps], axis=0), _tri2(2 * PAGE))
    for b in reversed(range(nb)):
        w = jnp.exp(zs[b] - suf[b * rows:(b + 1) * rows] - _lanes(carry, 2 * PAGE)).astype(BF16)
        carry = carry + jnp.sum(sps[b], axis=-1, keepdims=True)
        for i in range(2):
            acc = acc + _dot_nt(w[:, i * PAGE:(i + 1) * PAGE], vbuf[slot, 2 * b + i].astype(BF16))
    o_ref[0] = acc


def _sb_decode(layer, pt, q_bd, kn, vn, cache_k, cache_v):
    bs, rows, _ = q_bd.shape
    n_pages = pt.shape[1]
    assert n_pages % 2 == 0
    blk = lambda r, c: pl.BlockSpec((1, r, c), lambda i, pt: (i, 0, 0))
    return pl.pallas_call(
        functools.partial(_sb_decode_kernel, layer),
        grid_spec=pltpu.PrefetchScalarGridSpec(
            num_scalar_prefetch=1,
            grid=(bs,),
            in_specs=[blk(rows, PAGE), blk(PAGE, PAGE), blk(PAGE, PAGE),
                      pl.BlockSpec(memory_space=pl.ANY), pl.BlockSpec(memory_space=pl.ANY)],
            out_specs=blk(rows, PAGE),
            scratch_shapes=[pltpu.VMEM((2, n_pages, PAGE, PAGE), F32), pltpu.VMEM((2, n_pages, PAGE, PAGE), F32),
                            pltpu.SemaphoreType.DMA((2,)), pltpu.SemaphoreType.DMA((2,))]),
        out_shape=jax.ShapeDtypeStruct((bs, rows, PAGE), F32),
        compiler_params=_cparams("arbitrary"),
        name="sb_decode",
    )(pt, q_bd, kn, vn, cache_k, cache_v)


MLA_SCALE_LOG2 = MLA_SCALE * LOG2E


def _mla_prompt_kernel(q_ref, kvt_ref, lat_ref, o_ref, acc_ref, m_ref):
    qi = pl.program_id(1)
    rows = q_ref.shape[2]
    tk = kvt_ref.shape[3]
    tq = rows // MLA_H
    sub = min(MLA_SUB, rows)
    acc_ref[...] = jnp.zeros_like(acc_ref)
    m_ref[...] = jnp.full_like(m_ref, NEG_INF)
    ones = jnp.ones((tk, MLA_C), BF16)
    rowi = lax.broadcasted_iota(jnp.int32, (sub, tk), 0)
    col = lax.broadcasted_iota(jnp.int32, (sub, tk), 1)

    def block(kb, diag):
        kv = kvt_ref[0, kb]
        la = jnp.concatenate([lat_ref[0, kb], ones], axis=1)
        nsb = rows // sub
        ts = [_dot(q_ref[0, 0, sb * sub:(sb + 1) * sub, :], kv) * MLA_SCALE_LOG2 for sb in range(nsb)]
        for sb in range(nsb):
            r = slice(sb * sub, (sb + 1) * sub)
            t = ts[sb]
            if diag:
                t = jnp.where(col <= lax.rem(rowi + sb * sub, tq) + (qi * tq - kb * tk), t, NEG_INF)
            m_prev = m_ref[r, :]
            m_new = jnp.maximum(m_prev, jnp.max(t, axis=-1, keepdims=True))
            p = jnp.exp2(t - _lanes(m_new, tk))
            alpha = jnp.exp2(m_prev - m_new)
            acc_ref[r, :] = _lanes(alpha, 2 * MLA_C) * acc_ref[r, :] + _dot(p.astype(BF16), la)
            m_ref[r, :] = m_new

    n_full = (qi * tq + 1) // tk
    n_all = (qi * tq + tq + tk - 1) // tk

    def full_body(kb, _):
        block(kb, False)
        return 0

    def diag_body(kb, _):
        block(kb, True)
        return 0

    lax.fori_loop(0, n_full, full_body, 0)
    lax.fori_loop(n_full, n_all, diag_body, 0)
    acc = acc_ref[...]
    o_ref[0, 0] = (acc[:, :MLA_C] / acc[:, MLA_C:]).astype(BF16)


def _mla_prompt(q, kvt, lat):
    b, nq, rows, dq = q.shape
    nk, tk = kvt.shape[1], kvt.shape[3]
    return pl.pallas_call(
        _mla_prompt_kernel,
        grid=(b, nq),
        in_specs=[pl.BlockSpec((1, 1, rows, dq), lambda i, j: (i, j, 0, 0)),
                  pl.BlockSpec((1, nk, dq, tk), lambda i, j: (i, 0, 0, 0)),
                  pl.BlockSpec((1, nk, tk, MLA_C), lambda i, j: (i, 0, 0, 0))],
        out_specs=pl.BlockSpec((1, 1, rows, MLA_C), lambda i, j: (i, j, 0, 0)),
        out_shape=jax.ShapeDtypeStruct((b, nq, rows, MLA_C), BF16),
        scratch_shapes=[pltpu.VMEM((rows, 2 * MLA_C), F32), pltpu.VMEM((rows, 128), F32)],
        compiler_params=_cparams("arbitrary", "arbitrary"),
        name="mla_prompt",
    )(q, kvt, lat)


def _mla_decode_kernel(layer, pt_ref, qa_ref, qr_ref, ln_ref, rn_ref, cl_ref, cr_ref, o_ref,
                       lbuf, rbuf, lsem, rsem):
    n_pages = lbuf.shape[1]
    slot = _paged_step(pt_ref, n_pages, layer, [(cl_ref, lbuf, lsem), (cr_ref, rbuf, rsem)])
    qa = qa_ref[0]
    qr = qr_ref[0]
    rows = qa.shape[0]
    nt = rows // MLA_H
    t_row = lax.rem(lax.broadcasted_iota(jnp.int32, (rows, PAGE), 0), nt)
    mask = lax.broadcasted_iota(jnp.int32, (rows, PAGE), 1) <= t_row
    lats = [ln_ref[0]] + [lbuf[slot, p].astype(BF16) for p in range(n_pages)]
    krs = [rn_ref[0]] + [rbuf[slot, p].astype(BF16) for p in range(n_pages)]
    ts = [(_dot_nt(qa, lat) + _dot(qr, kr)) * MLA_SCALE_LOG2 for lat, kr in zip(lats, krs)]
    ts[0] = jnp.where(mask, ts[0], NEG_INF)
    m = functools.reduce(jnp.maximum, ts)
    m = jnp.max(m, axis=-1, keepdims=True)
    ps = [jnp.exp2(t - m) for t in ts]
    l = jnp.sum(functools.reduce(jnp.add, ps), axis=-1, keepdims=True)
    acc = functools.reduce(jnp.add, [_dot(p.astype(BF16), lat) for p, lat in zip(ps, lats)])
    o_ref[0] = (acc / l).astype(BF16)


def _mla_decode(layer, pt, qa, qr, ln, rn, cache_l, cache_r):
    bs, rows, _ = qa.shape
    n_pages = pt.shape[1]
    blk = lambda r, c: pl.BlockSpec((1, r, c), lambda i, pt: (i, 0, 0))
    return pl.pallas_call(
        functools.partial(_mla_decode_kernel, layer),
        grid_spec=pltpu.PrefetchScalarGridSpec(
            num_scalar_prefetch=1,
            grid=(bs,),
            in_specs=[blk(rows, MLA_C), blk(rows, MLA_ROPE), blk(PAGE, MLA_C), blk(MLA_ROPE, PAGE),
                      pl.BlockSpec(memory_space=pl.ANY), pl.BlockSpec(memory_space=pl.ANY)],
            out_specs=blk(rows, MLA_C),
            scratch_shapes=[pltpu.VMEM((2, n_pages, PAGE, MLA_C), F32), pltpu.VMEM((2, n_pages, MLA_ROPE, PAGE), F32),
                            pltpu.SemaphoreType.DMA((2,)), pltpu.SemaphoreType.DMA((2,))]),
        out_shape=jax.ShapeDtypeStruct((bs, rows, MLA_C), BF16),
        compiler_params=_cparams("arbitrary"),
        name="mla_decode",
    )(pt, qa, qr, ln, rn, cache_l, cache_r)


def _merge_kernel(x_ref, g_ref, ob_ref, ol_ref, gmix_ref, wg_ref, bg_ref, wa_ref, wb_ref, bdv_ref, wc_ref,
                  wo_ref, gmem_ref, wmq_ref, x1_ref, qm_ref):
    x = x_ref[...]
    xn = _rms(x, gmix_ref[...]).astype(BF16)
    gate = _sigmoid(_dot(xn, wg_ref[...]) + bg_ref[...])
    ya = _dot(g_ref[...], wa_ref[...])
    yb = _dot(ob_ref[...], wb_ref[...])
    yc = _dot(_dot(ol_ref[...], bdv_ref[...]).astype(BF16), wc_ref[...])
    merged = gate[:, :D_MODEL] * ya + gate[:, D_MODEL:2 * D_MODEL] * yb + gate[:, 2 * D_MODEL:] * yc
    x1 = x + _dot(merged.astype(BF16), wo_ref[...])
    x1_ref[...] = x1
    qm_ref[...] = _dot(_rms(x1, gmem_ref[...]).astype(BF16), wmq_ref[...]).astype(BF16)


def _merge(x, g, ob, ol, gmix, wg, bg, wa, wb, bdv, wc, wo, gmem, wmq):
    n = x.shape[0]
    tm = _row_tile(n, 256)
    row = lambda w: pl.BlockSpec((tm, w), lambda i: (i, 0))
    ws = [gmix, wg, bg, wa, wb, bdv, wc, wo, gmem, wmq]
    return pl.pallas_call(
        _merge_kernel,
        grid=(n // tm,),
        in_specs=[row(D_MODEL), row(LRU_W), row(512), row(1024)] + [_full(w.shape) for w in ws],
        out_specs=[row(D_MODEL), row(512)],
        out_shape=[jax.ShapeDtypeStruct((n, D_MODEL), F32), jax.ShapeDtypeStruct((n, 512), BF16)],
        compiler_params=_cparams("arbitrary"),
        name="merge",
    )(x, g, ob, ol, *ws)


def _memkv_kernel(m_ref, g_ref, w_ref, o_ref):
    o_ref[0] = _dot(_rms(m_ref[...], g_ref[0]).astype(BF16), w_ref[0])


def _memkv(mem, g_memkv, w_kv):
    nl, _, wout = w_kv.shape
    rows = mem.shape[0]
    return pl.pallas_call(
        _memkv_kernel,
        grid=(nl,),
        in_specs=[_full(mem.shape), pl.BlockSpec((1, 1, D_MODEL), lambda l: (l, 0, 0)),
                  pl.BlockSpec((1, D_MODEL, wout), lambda l: (l, 0, 0))],
        out_specs=pl.BlockSpec((1, rows, wout), lambda l: (l, 0, 0)),
        out_shape=jax.ShapeDtypeStruct((nl, rows, wout), F32),
        compiler_params=_cparams("arbitrary"),
        name="memkv",
    )(mem, g_memkv, w_kv)


def _mem_heads(q, mk, mv):
    outs = []
    for h in range(MEM_H):
        sl = slice(h * MEM_D, (h + 1) * MEM_D)
        s = _dot_nt(q[:, sl], mk[:, sl]) * MEM_SCALE
        e = jnp.exp(s - jnp.max(s, axis=-1, keepdims=True))
        p = e / jnp.sum(e, axis=-1, keepdims=True)
        outs.append(_dot(p.astype(BF16), mv[:, sl]))
    return jnp.concatenate(outs, axis=1).astype(BF16)


def _memattn_kernel(q_ref, mk_ref, mv_ref, o_ref):
    o_ref[0] = _mem_heads(q_ref[0], mk_ref[0].astype(BF16), mv_ref[0].astype(BF16))


def _memattn(q, mk, mv, tq):
    b, t, w = q.shape
    ml = mk.shape[1]
    return pl.pallas_call(
        _memattn_kernel,
        grid=(b, t // tq),
        in_specs=[pl.BlockSpec((1, tq, w), lambda i, j: (i, j, 0)),
                  pl.BlockSpec((1, ml, w), lambda i, j: (i, 0, 0)),
                  pl.BlockSpec((1, ml, w), lambda i, j: (i, 0, 0))],
        out_specs=pl.BlockSpec((1, tq, w), lambda i, j: (i, j, 0)),
        out_shape=jax.ShapeDtypeStruct((b, t, w), BF16),
        compiler_params=_cparams("arbitrary", "arbitrary"),
        name="memattn",
    )(q, mk, mv)


def _memattn_cache_kernel(q_ref, mk_ref, mv_ref, o_ref):
    q = q_ref[0]
    outs = []
    for h in range(MEM_H):
        s = _dot_nt(q[:, h * MEM_D:(h + 1) * MEM_D], mk_ref[0, 0, :, h, :].astype(BF16)) * MEM_SCALE
        e = jnp.exp(s - jnp.max(s, axis=-1, keepdims=True))
        p = e / jnp.sum(e, axis=-1, keepdims=True)
        outs.append(_dot(p.astype(BF16), mv_ref[0, 0, :, h, :].astype(BF16)))
    o_ref[0] = jnp.concatenate(outs, axis=1).astype(BF16)


def _memattn_cache(layer, q, cache_k, cache_v):
    bs, t, w = q.shape
    blk = (1, 1) + cache_k.shape[2:]
    return pl.pallas_call(
        _memattn_cache_kernel,
        grid=(bs,),
        in_specs=[pl.BlockSpec((1, t, w), lambda i: (i, 0, 0)),
                  pl.BlockSpec(blk, lambda i: (layer, i, 0, 0, 0)),
                  pl.BlockSpec(blk, lambda i: (layer, i, 0, 0, 0))],
        out_specs=pl.BlockSpec((1, t, w), lambda i: (i, 0, 0)),
        out_shape=jax.ShapeDtypeStruct((bs, t, w), BF16),
        compiler_params=_cparams("arbitrary"),
        name="memattn_cache",
    )(q, cache_k, cache_v)


def _mlp_kernel(final,x_ref, am_ref, wmo_ref, gmlp_ref, wup_ref, wdn_ref, gfin_ref, *rest):
    if final:
        o_ref, y_ref, x1_s, xn_s, acc_s = rest
    else:
        o_ref, x1_s, xn_s, acc_s = rest
    j = pl.program_id(1)

    @pl.when(j == 0)
    def _():
        x1 = x_ref[...] + _dot(am_ref[...], wmo_ref[...])
        x1_s[...] = x1
        xn_s[...] = _rms(x1, gmlp_ref[...]).astype(BF16)
        acc_s[...] = jnp.zeros_like(acc_s)

    h = jnp.maximum(_dot(xn_s[...], wup_ref[...]), 0.0)
    acc_s[...] += _dot((h * h).astype(BF16), wdn_ref[...])

    @pl.when(j == pl.num_programs(1) - 1)
    def _():
        out = x1_s[...] + acc_s[...]
        o_ref[...] = out
        if final:
            y_ref[...] = _rms(out, gfin_ref[...])


def _mlp(x, am, wmo, gmlp, wup, wdn, gfin, final):
    n = x.shape[0]
    tm = _row_tile(n, 512)
    tf = 1024
    row = lambda w: pl.BlockSpec((tm, w), lambda i, j: (i, 0))
    cst = lambda s: pl.BlockSpec(s, lambda i, j: (0, 0))
    n_out = 2 if final else 1
    outs = pl.pallas_call(
        functools.partial(_mlp_kernel, final),
        grid=(n // tm, D_FF // tf),
        in_specs=[row(D_MODEL), row(512), cst(wmo.shape), cst(gmlp.shape),
                  pl.BlockSpec((D_MODEL, tf), lambda i, j: (0, j)),
                  pl.BlockSpec((tf, D_MODEL), lambda i, j: (j, 0)), cst(gfin.shape)],
        out_specs=[row(D_MODEL)] * n_out,
        out_shape=[jax.ShapeDtypeStruct((n, D_MODEL), F32)] * n_out,
        scratch_shapes=[pltpu.VMEM((tm, D_MODEL), F32), pltpu.VMEM((tm, D_MODEL), BF16),
                        pltpu.VMEM((tm, D_MODEL), F32)],
        compiler_params=_cparams("arbitrary", "arbitrary"),
        name="mlp",
    )(x, am, wmo, gmlp, wup, wdn, gfin)
    return outs


def _rope_tables(pos):
    half = MLA_ROPE // 2
    inv = ROPE_THETA ** (-jnp.arange(half, dtype=F32) / half)
    ang = pos.astype(F32)[:, None] * inv[None, :]
    cos, sin = jnp.cos(ang), jnp.sin(ang)
    c = jnp.concatenate([cos, cos], axis=1)
    s = jnp.concatenate([-sin, sin], axis=1)
    pad = jnp.zeros((pos.shape[0], 128 - MLA_ROPE), F32)
    tabk = jnp.concatenate([c, pad, s, pad], axis=1)
    tabq = jnp.concatenate([jnp.tile(c, (1, MLA_H)), jnp.tile(s, (1, MLA_H))], axis=1)
    return tabk, tabq


def _swap_halves(w):
    half = w.shape[-1] // 2
    return jnp.concatenate([w[..., half:], w[..., :half]], axis=-1)


def _block_diag(w):
    n, a, b = w.shape
    return jnp.einsum('nab,nm->namb', w, jnp.eye(n, dtype=w.dtype)).reshape(n * a, n * b)


def kernel(x_prompt, x_sample, mem_prompt, cache_mla_latent, cache_mla_krope, cache_sb_k, cache_sb_v, cache_mem_k, cache_mem_v, state_lru_h, state_conv, page_table, g_mix, w_in, b_gate, conv_w, conv_b, lru_wa, lru_ba, lru_wx, lru_bx, lru_lambda, mla_gq, mla_wuq, mla_gkv, mla_wuk, mla_wuv, w_lru_o, w_sb_o, w_mla_o, w_out, g_mem, g_memkv, w_mq, w_mk, w_mv, w_mo, g_mlp, w_up, w_down, g_final):
    bp, tp, _ = x_prompt.shape
    bs, ts, _ = x_sample.shape
    depth = w_in.shape[0]
    n_pool = cache_sb_k.shape[1]
    n_pages = page_table.shape[1]
    past = n_pages * PAGE
    mem_len = mem_prompt.shape[1]
    n_p, n_s = bp * tp, bs * ts
    assert ts * SB_HEADS == 32 and ts * MLA_H == 32

    pos = jnp.concatenate([jnp.tile(jnp.arange(tp, dtype=jnp.int32), bp),
                           jnp.tile(past + jnp.arange(ts, dtype=jnp.int32), bs)])
    tabk, tabq = _rope_tables(pos)

    tq_sb = _row_tile(tp, SB_TQ)
    nq_sb = tp // tq_sb
    tq_mla = _row_tile(tp, MLA_TQ)
    tk_mla = _row_tile(tp, MLA_TK)
    nq_mla, nk_mla = tp // tq_mla, tp // tk_mla

    ck = cache_sb_k.transpose(0, 1, 3, 4, 2).reshape(depth, n_pool, SB_KVH * SB_D, PAGE)
    cv = cache_sb_v.transpose(0, 1, 3, 4, 2).reshape(depth, n_pool, SB_KVH * SB_D, PAGE)
    cr = cache_mla_krope.transpose(0, 1, 3, 2)

    mem_kv = _memkv(mem_prompt.reshape(bp * mem_len, D_MODEL), g_memkv[:, None, :],
                    jnp.concatenate([w_mk, w_mv], axis=2).astype(BF16))
    p_mk = mem_kv[:, :, :MEM_H * MEM_D].reshape(depth, bp, mem_len, MEM_H * MEM_D)
    p_mv = mem_kv[:, :, MEM_H * MEM_D:].reshape(depth, bp, mem_len, MEM_H * MEM_D)

    x = jnp.concatenate([x_prompt.reshape(n_p, D_MODEL), x_sample.reshape(n_s, D_MODEL)], axis=0)
    outs = {k: [] for k in ("p_lat", "p_kr", "p_k", "p_v", "p_h", "p_conv",
                            "s_lat", "s_kr", "s_k", "s_v", "s_h", "s_conv")}
    row2 = lambda v: v.reshape(1, -1)
    zpad = lambda a, n: jnp.pad(a, ((0, 0), (0, n - a.shape[1]), (0, 0)))
    y = None
    for l in range(depth):
        wl = w_in[l]
        w2 = wl[:, :2 * LRU_W].astype(BF16)
        kr_w = wl[:, 3200:3232]
        zc = jnp.zeros((D_MODEL, 128 - MLA_ROPE), F32)
        wp = jnp.concatenate([wl[:, 2048:3200], kr_w, zc, _swap_halves(kr_w), zc], axis=1).astype(BF16)
        wg = wl[:, 3232:].astype(BF16)
        wuq3 = mla_wuq[l].reshape(MLA_QR, MLA_H, MLA_NOPE + MLA_ROPE)
        rope_w = wuq3[:, :, MLA_NOPE:]
        wuq = jnp.concatenate([wuq3[:, :, :MLA_NOPE].reshape(MLA_QR, -1), rope_w.reshape(MLA_QR, -1),
                               _swap_halves(rope_w).reshape(MLA_QR, -1)], axis=1).astype(BF16)
        bdk = _block_diag(jnp.transpose(mla_wuk[l], (1, 2, 0))).astype(BF16)
        bdv = _block_diag(jnp.transpose(mla_wuv[l], (1, 0, 2))).astype(BF16)
        bda = _block_diag(lru_wa[l]).astype(BF16)
        bdx = _block_diag(lru_wx[l]).astype(BF16)
        gmix = row2(g_mix[l])
        lru_w = [gmix, w2, conv_w[l], row2(conv_b[l]), bda, row2(lru_ba[l]), bdx, row2(lru_bx[l]),
                 row2(lru_lambda[l])]

        qsb, k_new, v_new, lat_new, kr_new, qabs, qrope = _proj(
            x, gmix, wp, row2(mla_gq[l]), wuq, row2(mla_gkv[l]), bdk, tabk, tabq)

        g_p, h_p, conv_p = _lru_prompt(x[:n_p].reshape(bp, tp, D_MODEL), *lru_w)
        xs_tb = x[n_p:].reshape(bs, ts, D_MODEL).transpose(1, 0, 2).reshape(n_s, D_MODEL)
        tail = state_conv[l].transpose(1, 0, 2).reshape((CONV_W - 1) * bs, LRU_W)
        g_s, h_s, conv_s = _lru_sample(past, xs_tb, state_lru_h[l], tail, *lru_w)
        g_s = g_s.reshape(ts, bs, LRU_W).transpose(1, 0, 2).reshape(n_s, LRU_W)
        g_all = jnp.concatenate([g_p.reshape(n_p, LRU_W), g_s], axis=0)
        outs["p_h"].append(h_p.reshape(bp, LRU_W))
        outs["p_conv"].append(conv_p[:, 8 - (CONV_W - 1):, :])
        outs["s_h"].append(h_s)
        outs["s_conv"].append(conv_s.reshape(CONV_W - 1, bs, LRU_W).transpose(1, 0, 2))

        kb, vb = k_new.astype(BF16), v_new.astype(BF16)
        q5 = qsb[:n_p].reshape(bp, nq_sb, tq_sb, SB_KVH, SB_G, SB_D).transpose(0, 3, 1, 4, 2, 5)
        q5 = q5.reshape(bp, SB_KVH, nq_sb, SB_G * tq_sb, SB_D)
        k5 = kb[:n_p].reshape(bp, nq_sb, tq_sb, SB_KVH, SB_D)
        v5 = vb[:n_p].reshape(bp, nq_sb, tq_sb, SB_KVH, SB_D)
        ob_p = _sb_prompt(q5, k5.transpose(0, 3, 1, 4, 2), v5.transpose(0, 3, 1, 2, 4))
        ob_p = ob_p.reshape(bp, SB_KVH, nq_sb, SB_G, tq_sb, SB_D).transpose(0, 2, 4, 1, 3, 5)
        ob_p = ob_p.reshape(n_p, SB_HEADS * SB_D).astype(BF16)

        qs = qsb[n_p:].reshape(bs, ts, SB_KVH, SB_G, SB_D).transpose(0, 2, 3, 1, 4)
        q_bd = jnp.einsum('bhgtd,hk->bhgtkd', qs, jnp.eye(SB_KVH, dtype=BF16))
        q_bd = q_bd.reshape(bs, SB_HEADS * ts, SB_KVH * SB_D)
        kn = zpad(kb[n_p:].reshape(bs, ts, SB_KVH * SB_D), PAGE).transpose(0, 2, 1)
        vn = zpad(vb[n_p:].reshape(bs, ts, SB_KVH * SB_D), PAGE).transpose(0, 2, 1)
        ob_s = _sb_decode(l, page_table, q_bd, kn, vn, ck, cv)
        ob_s = ob_s.reshape(bs, SB_KVH, SB_G, ts, SB_KVH, SB_D)
        ob_s = jnp.stack([ob_s[:, h, :, :, h, :] for h in range(SB_KVH)], axis=1)
        ob_s = ob_s.transpose(0, 3, 1, 2, 4).reshape(n_s, SB_HEADS * SB_D).astype(BF16)
        ob = jnp.concatenate([ob_p, ob_s], axis=0)

        latb, krb = lat_new.astype(BF16), kr_new.astype(BF16)
        qa4 = qabs[:n_p].reshape(bp, nq_mla, tq_mla, MLA_H, MLA_C).transpose(0, 1, 3, 2, 4)
        qr4 = qrope[:n_p].reshape(bp, nq_mla, tq_mla, MLA_H, MLA_ROPE).transpose(0, 1, 3, 2, 4)
        qcat = jnp.concatenate([qa4, qr4, jnp.zeros(qa4.shape[:-1] + (256 - MLA_C - MLA_ROPE,), BF16)], axis=-1)
        qcat = qcat.reshape(bp, nq_mla, MLA_H * tq_mla, 256)
        lat4 = latb[:n_p].reshape(bp, nk_mla, tk_mla, MLA_C)
        kv4 = jnp.concatenate([lat4, krb[:n_p].reshape(bp, nk_mla, tk_mla, MLA_ROPE),
                               jnp.zeros((bp, nk_mla, tk_mla, 256 - MLA_C - MLA_ROPE), BF16)], axis=-1)
        ol_p = _mla_prompt(qcat, kv4.transpose(0, 1, 3, 2), lat4)
        ol_p = ol_p.reshape(bp, nq_mla, MLA_H, tq_mla, MLA_C).transpose(0, 1, 3, 2, 4).reshape(n_p, MLA_H * MLA_C)

        qa_s = qabs[n_p:].reshape(bs, ts, MLA_H, MLA_C).transpose(0, 2, 1, 3).reshape(bs, MLA_H * ts, MLA_C)
        qr_s = qrope[n_p:].reshape(bs, ts, MLA_H, MLA_ROPE).transpose(0, 2, 1, 3).reshape(bs, MLA_H * ts, MLA_ROPE)
        ln = zpad(latb[n_p:].reshape(bs, ts, MLA_C), PAGE)
        rn = zpad(krb[n_p:].reshape(bs, ts, MLA_ROPE), PAGE).transpose(0, 2, 1)
        ol_s = _mla_decode(l, page_table, qa_s, qr_s, ln, rn, cache_mla_latent, cr)
        ol_s = ol_s.reshape(bs, MLA_H, ts, MLA_C).transpose(0, 2, 1, 3).reshape(n_s, MLA_H * MLA_C)
        ol = jnp.concatenate([ol_p, ol_s], axis=0)

        outs["p_lat"].append(lat_new[:n_p].reshape(bp, tp, MLA_C))
        outs["p_kr"].append(kr_new[:n_p].reshape(bp, tp, MLA_ROPE))
        outs["p_k"].append(k_new[:n_p].reshape(bp, tp, SB_KVH, SB_D))
        outs["p_v"].append(v_new[:n_p].reshape(bp, tp, SB_KVH, SB_D))
        outs["s_lat"].append(lat_new[n_p:].reshape(bs, ts, MLA_C))
        outs["s_kr"].append(kr_new[n_p:].reshape(bs, ts, MLA_ROPE))
        outs["s_k"].append(k_new[n_p:].reshape(bs, ts, SB_KVH, SB_D))
        outs["s_v"].append(v_new[n_p:].reshape(bs, ts, SB_KVH, SB_D))

        x1, qm = _merge(x, g_all, ob, ol, gmix, wg, row2(b_gate[l]), w_lru_o[l].astype(BF16),
                        w_sb_o[l].astype(BF16), bdv, w_mla_o[l].astype(BF16), w_out[l].astype(BF16),
                        row2(g_mem[l]), w_mq[l].astype(BF16))
        am_p = _memattn(qm[:n_p].reshape(bp, tp, MEM_H * MEM_D), p_mk[l], p_mv[l], _row_tile(tp, 512))
        am_s = _memattn_cache(l, qm[n_p:].reshape(bs, ts, MEM_H * MEM_D), cache_mem_k, cache_mem_v)
        am = jnp.concatenate([am_p.reshape(n_p, -1), am_s.reshape(n_s, -1)], axis=0)
        final = l == depth - 1
        res = _mlp(x1, am, w_mo[l].astype(BF16), row2(g_mlp[l]), w_up[l].astype(BF16), w_down[l].astype(BF16),
                   row2(g_final), final)
        x = res[0]
        if final:
            y = res[1]

    st = {k: jnp.stack(v) for k, v in outs.items()}
    mshape = (depth, bp, mem_len, MEM_H, MEM_D)
    return (y[:n_p].reshape(bp, tp, D_MODEL), y[n_p:].reshape(bs, ts, D_MODEL),
            st["p_lat"], st["p_kr"], st["p_k"], st["p_v"], st["p_h"], st["p_conv"],
            p_mk.reshape(mshape), p_mv.reshape(mshape),
            st["s_lat"], st["s_kr"], st["s_k"], st["s_v"], st["s_h"], st["s_conv"])
```

```python
import functools

import numpy as np
import jax
import jax.numpy as jnp
from jax import lax
from jax.experimental import pallas as pl
from jax.experimental.pallas import tpu as pltpu

F32 = jnp.float32
BF16 = jnp.bfloat16

D_MODEL = 1024
PAGE = 128
LRU_W = 1024
LRU_BLOCKS = 16
CONV_W = 4
LRU_C = 8.0
SB_HEADS = 8
SB_KVH = 2
SB_G = 4
SB_D = 64
SB_SCALE = SB_D ** -0.5
MLA_H = 8
MLA_QR = 256
MLA_C = 128
MLA_NOPE = 64
MLA_ROPE = 32
MLA_V = 64
MLA_SCALE = (MLA_NOPE + MLA_ROPE) ** -0.5
ROPE_THETA = 10000.0
MEM_H = 4
MEM_D = 128
MEM_SCALE = MEM_D ** -0.5
D_FF = 4096
EPS = 1e-6

SB_TQ = 256
MLA_TQ = 128
MLA_TK = 256
SB_SUB = 1024
MLA_SUB = 512
SB_DEAD_MASS = 105.0
LOG2E = 1.4426950408889634

VMEM_LIMIT_V7X = 56 * 1024 * 1024
NEG_INF = float("-inf")


def _cparams(*sem):
    return pltpu.CompilerParams(dimension_semantics=sem, vmem_limit_bytes=VMEM_LIMIT_V7X)


def _rms(x, g):
    return x * lax.rsqrt(jnp.mean(x * x, axis=-1, keepdims=True) + EPS) * g


def _dot(a, b):
    return jnp.dot(a, b, preferred_element_type=F32)


def _dot_nt(a, b):
    return lax.dot_general(a, b, (((1,), (1,)), ((), ())), preferred_element_type=F32)


def _softplus(z):
    return jnp.maximum(z, 0.0) + jnp.log1p(jnp.exp(-jnp.abs(z)))


def _softplus_fast(z):
    return jnp.maximum(z, 0.0) + jnp.log(1.0 + jnp.exp(-jnp.abs(z)))


def _sigmoid(z):
    return 1.0 / (1.0 + jnp.exp(-z))


def _full(shape):
    n = len(shape)
    return pl.BlockSpec(shape, lambda *_: (0,) * n)


def _row_tile(n, want):
    t = min(want, n)
    while n % t:
        t //= 2
    return t


def _proj_kernel(x_ref, gmix_ref, wp_ref, gq_ref, wuq_ref, gkv_ref, bdk_ref, tabk_ref, tabq_ref,
                 qsb_ref, k_ref, v_ref, lat_ref, kr_ref, qabs_ref, qrope_ref):
    xn = _rms(x_ref[...], gmix_ref[...]).astype(BF16)
    u = _dot(xn, wp_ref[...])
    qsb_ref[...] = (u[:, 0:512] * SB_SCALE).astype(BF16)
    k_ref[...] = u[:, 512:640]
    v_ref[...] = u[:, 640:768]
    lat_ref[...] = _rms(u[:, 1024:1152], gkv_ref[...])
    tabk = tabk_ref[...]
    kr = u[:, 1152:1280] * tabk[:, :128] + u[:, 1280:1408] * tabk[:, 128:]
    kr_ref[...] = kr[:, :MLA_ROPE]
    cqn = _rms(u[:, 768:1024], gq_ref[...]).astype(BF16)
    qc = _dot(cqn, wuq_ref[...])
    qabs_ref[...] = _dot(qc[:, :512].astype(BF16), bdk_ref[...]).astype(BF16)
    tabq = tabq_ref[...]
    qrope_ref[...] = (qc[:, 512:768] * tabq[:, :256] + qc[:, 768:] * tabq[:, 256:]).astype(BF16)


def _proj(x, gmix, wp, gq, wuq, gkv, bdk, tabk, tabq):
    n = x.shape[0]
    tm = _row_tile(n, 512)
    row = lambda w: pl.BlockSpec((tm, w), lambda i: (i, 0))
    outs = [(512, BF16), (128, F32), (128, F32), (128, F32), (MLA_ROPE, F32), (1024, BF16), (256, BF16)]
    return pl.pallas_call(
        _proj_kernel,
        grid=(n // tm,),
        in_specs=[row(D_MODEL), _full(gmix.shape), _full(wp.shape), _full(gq.shape), _full(wuq.shape),
                  _full(gkv.shape), _full(bdk.shape), row(256), row(512)],
        out_specs=[row(w) for w, _ in outs],
        out_shape=[jax.ShapeDtypeStruct((n, w), dt) for w, dt in outs],
        compiler_params=_cparams("arbitrary"),
        name="proj",
    )(x, gmix, wp, gq, wuq, gkv, bdk, tabk, tabq)


def _lru_gates(xc, bda_ref, ba_ref, bdx_ref, bx_ref, lam_ref):
    xb = xc.astype(BF16)
    r = _sigmoid(_dot(xb, bda_ref[...]) + ba_ref[...])
    i = _sigmoid(_dot(xb, bdx_ref[...]) + bx_ref[...])
    log_a = -LRU_C * r * _softplus(-lam_ref[...])
    a = jnp.exp(log_a)
    mult = jnp.sqrt(jnp.maximum(1.0 - a * a, 0.0))
    return a, mult, i * xc


def _gelu_tanh(x):
    c = np.float32(np.sqrt(2.0 / np.pi))
    return 0.5 * x * (1.0 + jnp.tanh(c * (x + 0.044715 * (x * x * x))))


def _lru_prompt_kernel(x_ref, gmix_ref, w2_ref, cw_ref, cb_ref, bda_ref, ba_ref, bdx_ref, bx_ref, lam_ref,
                       g_ref, hlast_ref, conv_ref, ubuf, hc):
    t = pl.program_id(1)
    tc = x_ref.shape[1]

    @pl.when(t == 0)
    def _():
        ubuf[0:8, :] = jnp.zeros((8, LRU_W), F32)
        hc[...] = jnp.zeros_like(hc)

    xn = _rms(x_ref[0], gmix_ref[...]).astype(BF16)
    u = _dot(xn, w2_ref[...])
    ubuf[8:8 + tc, :] = u[:, :LRU_W]
    cw = cw_ref[...]
    xc = cb_ref[...]
    for k in range(CONV_W):
        xc = xc + ubuf[5 + k:5 + k + tc, :] * cw[k:k + 1, :]
    a, mult, ix = _lru_gates(xc, bda_ref, ba_ref, bdx_ref, bx_ref, lam_ref)
    rows = lax.broadcasted_iota(jnp.int32, (tc, LRU_W), 0)
    mult = jnp.where(rows == jnp.where(t == 0, 0, -1), 1.0, mult)
    b = mult * ix
    d = 1
    while d < tc:
        keep = rows >= d
        a_sh = jnp.where(keep, pltpu.roll(a, d, 0), 1.0)
        b_sh = jnp.where(keep, pltpu.roll(b, d, 0), 0.0)
        b = a * b_sh + b
        a = a * a_sh
        d *= 2
    h = a * hc[0:1, :] + b
    hc[0:1, :] = h[tc - 1:tc, :]
    hlast_ref[0] = h[tc - 1:tc, :]
    g_ref[0] = (h * _gelu_tanh(u[:, LRU_W:])).astype(BF16)
    tail = ubuf[tc:tc + 8, :]
    conv_ref[0] = tail
    ubuf[0:8, :] = tail


def _lru_prompt(xp, gmix, w2, cw, cb, bda, ba, bdx, bx, lam):
    b, t, _ = xp.shape
    tc = _row_tile(t, 256)
    assert tc >= 8
    ws = [gmix, w2, cw, cb, bda, ba, bdx, bx, lam]
    return pl.pallas_call(
        _lru_prompt_kernel,
        grid=(b, t // tc),
        in_specs=[pl.BlockSpec((1, tc, D_MODEL), lambda i, j: (i, j, 0))] + [_full(w.shape) for w in ws],
        out_specs=[pl.BlockSpec((1, tc, LRU_W), lambda i, j: (i, j, 0)),
                   pl.BlockSpec((1, 1, LRU_W), lambda i, j: (i, 0, 0)),
                   pl.BlockSpec((1, 8, LRU_W), lambda i, j: (i, 0, 0))],
        out_shape=[jax.ShapeDtypeStruct((b, t, LRU_W), BF16),
                   jax.ShapeDtypeStruct((b, 1, LRU_W), F32),
                   jax.ShapeDtypeStruct((b, 8, LRU_W), F32)],
        scratch_shapes=[pltpu.VMEM((tc + 8, LRU_W), F32), pltpu.VMEM((8, LRU_W), F32)],
        compiler_params=_cparams("arbitrary", "arbitrary"),
        name="lru_prompt",
    )(xp, *ws)


def _lru_sample_kernel(first_pos, x_ref, h0_ref, tail_ref, gmix_ref, w2_ref, cw_ref, cb_ref, bda_ref, ba_ref,
                       bdx_ref, bx_ref, lam_ref, g_ref, hlast_ref, conv_ref, ubuf):
    nb = h0_ref.shape[0]
    nt = x_ref.shape[0] // nb
    xn = _rms(x_ref[...], gmix_ref[...]).astype(BF16)
    u = _dot(xn, w2_ref[...])
    ubuf[0:(CONV_W - 1) * nb, :] = tail_ref[...]
    ubuf[(CONV_W - 1) * nb:, :] = u[:, :LRU_W]
    cw = cw_ref[...]
    xc = cb_ref[...]
    for k in range(CONV_W):
        xc = xc + ubuf[k * nb:(k + nt) * nb, :] * cw[k:k + 1, :]
    a, mult, ix = _lru_gates(xc, bda_ref, ba_ref, bdx_ref, bx_ref, lam_ref)
    gl = _gelu_tanh(u[:, LRU_W:])
    h = h0_ref[...]
    for t in range(nt):
        sl = slice(t * nb, (t + 1) * nb)
        m = mult[sl] if first_pos + t != 0 else jnp.ones_like(mult[sl])
        h = a[sl] * h + m * ix[sl]
        g_ref[sl, :] = (h * gl[sl]).astype(BF16)
    hlast_ref[...] = h
    conv_ref[...] = ubuf[nt * nb:(nt + CONV_W - 1) * nb, :]


def _lru_sample(first_pos, xs_tb, h0, tail, gmix, w2, cw, cb, bda, ba, bdx, bx, lam):
    n = xs_tb.shape[0]
    nb = h0.shape[0]
    args = [xs_tb, h0, tail, gmix, w2, cw, cb, bda, ba, bdx, bx, lam]
    return pl.pallas_call(
        functools.partial(_lru_sample_kernel, first_pos),
        grid=(1,),
        in_specs=[_full(a.shape) for a in args],
        out_specs=[_full((n, LRU_W)), _full((nb, LRU_W)), _full(((CONV_W - 1) * nb, LRU_W))],
        out_shape=[jax.ShapeDtypeStruct((n, LRU_W), BF16),
                   jax.ShapeDtypeStruct((nb, LRU_W), F32),
                   jax.ShapeDtypeStruct(((CONV_W - 1) * nb, LRU_W), F32)],
        scratch_shapes=[pltpu.VMEM((n + (CONV_W - 1) * nb, LRU_W), F32)],
        compiler_params=_cparams("arbitrary"),
        name="lru_sample",
    )(*args)


def _tri2(n):
    j = lax.rem(lax.broadcasted_iota(jnp.int32, (2 * n, n), 0), n)
    s = lax.broadcasted_iota(jnp.int32, (2 * n, n), 1)
    return jnp.where(j >= s, 1.0, 0.0).astype(BF16)


def _hi_lo(x):
    hi = x.astype(BF16)
    lo = (x - hi.astype(F32)).astype(BF16)
    return jnp.concatenate([hi, lo], axis=1)


def _lanes(x, n):
    return x if n == 128 else jnp.concatenate([x] * (n // 128), axis=1)


def _sb_weights(z, mask, tri2, carry):
    sp = _softplus_fast(z)
    if mask is not None:
        sp = jnp.where(mask, sp, 0.0)
    w = jnp.exp(z - _dot(_hi_lo(sp), tri2) - _lanes(carry, z.shape[1]))
    if mask is not None:
        w = jnp.where(mask, w, 0.0)
    return w, carry + jnp.sum(sp, axis=-1, keepdims=True)


def _sb_prompt_kernel(q_ref, kt_ref, v_ref, o_ref, acc_ref, carry_ref):
    qi = pl.program_id(2)
    rows = q_ref.shape[3]
    tk = kt_ref.shape[4]
    tq = rows // SB_G
    sub = min(SB_SUB, rows)
    tri2 = _tri2(tk)
    acc_ref[...] = jnp.zeros_like(acc_ref)
    carry_ref[...] = jnp.zeros_like(carry_ref)
    rowi = lax.broadcasted_iota(jnp.int32, (sub, tk), 0)
    col = lax.broadcasted_iota(jnp.int32, (sub, tk), 1)

    def block(kb, diag):
        kt = kt_ref[0, 0, kb]
        v = v_ref[0, 0, kb]
        for sb in range(rows // sub):
            r = slice(sb * sub, (sb + 1) * sub)
            z = _dot(q_ref[0, 0, 0, r, :], kt)
            mask = col < lax.rem(rowi + sb * sub, tq) if diag else None
            w, carry = _sb_weights(z, mask, tri2, carry_ref[r, :])
            carry_ref[r, :] = carry
            acc_ref[r, :] += _dot(w.astype(BF16), v)

    block(qi, True)

    def live():
        return jnp.min(carry_ref[...]) <= SB_DEAD_MASS

    def cond(st):
        return (st[0] < qi) & st[1]

    def body(st):
        block(qi - 1 - st[0], False)
        return st[0] + 1, live()

    lax.while_loop(cond, body, (jnp.int32(0), live()))
    o_ref[0, 0, 0] = acc_ref[...]


def _sb_prompt(q, kt, v):
    b, kvh, nq, rows, d = q.shape
    nk, tk = kt.shape[2], kt.shape[4]
    assert rows == SB_G * tk and nq == nk
    return pl.pallas_call(
        _sb_prompt_kernel,
        grid=(b, kvh, nq),
        in_specs=[pl.BlockSpec((1, 1, 1, rows, d), lambda i, h, j: (i, h, j, 0, 0)),
                  pl.BlockSpec((1, 1, nk, d, tk), lambda i, h, j: (i, h, 0, 0, 0)),
                  pl.BlockSpec((1, 1, nk, tk, d), lambda i, h, j: (i, h, 0, 0, 0))],
        out_specs=pl.BlockSpec((1, 1, 1, rows, d), lambda i, h, j: (i, h, j, 0, 0)),
        out_shape=jax.ShapeDtypeStruct(q.shape, F32),
        scratch_shapes=[pltpu.VMEM((rows, d), F32), pltpu.VMEM((rows, 128), F32)],
        compiler_params=_cparams("arbitrary", "arbitrary", "arbitrary"),
        name="sb_prompt",
    )(q, kt, v)


def _start_pages(pt_ref, b, first, count, layer, pairs, slot):
    def body(p, _):
        pg = pt_ref[b, first + p]
        for cache, buf, sem in pairs:
            pltpu.make_async_copy(cache.at[layer, pg], buf.at[slot, p], sem.at[slot]).start()
        return 0
    lax.fori_loop(0, count, body, 0)


def _wait_pages(count, layer, pairs, slot):
    def body(p, _):
        for cache, buf, sem in pairs:
            pltpu.make_async_copy(cache.at[layer, 0], buf.at[slot, p], sem.at[slot]).wait()
        return 0
    lax.fori_loop(0, count, body, 0)


def _paged_step(pt_ref, first, count, layer, pairs):
    b = pl.program_id(0)
    slot = lax.rem(b, 2)

    @pl.when(b == 0)
    def _():
        _start_pages(pt_ref, b, first, count, layer, pairs, slot)

    @pl.when(b + 1 < pl.num_programs(0))
    def _():
        _start_pages(pt_ref, b + 1, first, count, layer, pairs, 1 - slot)

    _wait_pages(count, layer, pairs, slot)
    return slot


def _sb_decode_kernel(layer, pt_ref, q_ref, kn_ref, vn_ref, ck_ref, cv_ref, o_ref,
                      kbuf, vbuf, kold, vold, ksem, vsem, osem):
    n_pages = pt_ref.shape[1]
    nb = n_pages // 2
    slot = _paged_step(pt_ref, n_pages - 2, 2, layer, [(ck_ref, kbuf, ksem), (cv_ref, vbuf, vsem)])
    seq = pl.program_id(0)
    q = q_ref[0]
    rows = q.shape[0]
    nt = rows // SB_HEADS
    tri2 = _tri2(2 * PAGE)
    t_row = lax.rem(lax.broadcasted_iota(jnp.int32, (rows, PAGE), 0), nt)
    mask = lax.broadcasted_iota(jnp.int32, (rows, PAGE), 1) < t_row
    w, carry = _sb_weights(_dot(q, kn_ref[0]), mask, _tri2(PAGE), jnp.zeros((rows, 128), F32))
    acc = _dot_nt(w.astype(BF16), vn_ref[0])

    def block(kpages, vpages, carry, acc):
        z = jnp.concatenate([_dot(q, k.astype(BF16)) for k in kpages], axis=1)
        w, carry = _sb_weights(z, None, tri2, carry)
        w = w.astype(BF16)
        for i in range(2):
            acc = acc + _dot_nt(w[:, i * PAGE:(i + 1) * PAGE], vpages[i].astype(BF16))
        return carry, acc

    carry, acc = block([kbuf[slot, 0], kbuf[slot, 1]], [vbuf[slot, 0], vbuf[slot, 1]], carry, acc)

    def cond(st):
        return (st[0] >= 0) & (jnp.min(st[1]) <= SB_DEAD_MASS)

    def body(st):
        blk, carry, acc = st
        copies = []
        for i in range(2):
            pg = pt_ref[seq, 2 * blk + i]
            for j, (cache, buf) in enumerate(((ck_ref, kold), (cv_ref, vold))):
                copies.append(pltpu.make_async_copy(cache.at[layer, pg], buf.at[i], osem.at[2 * i + j]))
        for cp in copies:
            cp.start()
        for cp in copies:
            cp.wait()
        carry, acc = block([kold[0], kold[1]], [vold[0], vold[1]], carry, acc)
        return blk - 1, carry, acc

    _, _, acc = lax.while_loop(cond, body, (jnp.int32(nb - 2), carry, acc))
    o_ref[0] = acc


def _sb_decode(layer, pt, q_bd, kn, vn, cache_k, cache_v):
    bs, rows, _ = q_bd.shape
    n_pages = pt.shape[1]
    assert n_pages % 2 == 0
    blk = lambda r, c: pl.BlockSpec((1, r, c), lambda i, pt: (i, 0, 0))
    page_buf = lambda *lead: pltpu.VMEM(lead + (PAGE, PAGE), F32)
    return pl.pallas_call(
        functools.partial(_sb_decode_kernel, layer),
        grid_spec=pltpu.PrefetchScalarGridSpec(
            num_scalar_prefetch=1,
            grid=(bs,),
            in_specs=[blk(rows, PAGE), blk(PAGE, PAGE), blk(PAGE, PAGE),
                      pl.BlockSpec(memory_space=pl.ANY), pl.BlockSpec(memory_space=pl.ANY)],
            out_specs=blk(rows, PAGE),
            scratch_shapes=[page_buf(2, 2), page_buf(2, 2), page_buf(2), page_buf(2),
                            pltpu.SemaphoreType.DMA((2,)), pltpu.SemaphoreType.DMA((2,)),
                            pltpu.SemaphoreType.DMA((4,))]),
        out_shape=jax.ShapeDtypeStruct((bs, rows, PAGE), F32),
        compiler_params=_cparams("arbitrary"),
        name="sb_decode",
    )(pt, q_bd, kn, vn, cache_k, cache_v)


MLA_SCALE_LOG2 = MLA_SCALE * LOG2E


def _mla_prompt_kernel(q_ref, kvt_ref, lat_ref, o_ref, acc_ref, m_ref):
    qi = pl.program_id(1)
    rows = q_ref.shape[2]
    tk = kvt_ref.shape[3]
    tq = rows // MLA_H
    sub = min(MLA_SUB, rows)
    acc_ref[...] = jnp.zeros_like(acc_ref)
    m_ref[...] = jnp.full_like(m_ref, NEG_INF)
    ones = jnp.ones((tk, MLA_C), BF16)
    rowi = lax.broadcasted_iota(jnp.int32, (sub, tk), 0)
    col = lax.broadcasted_iota(jnp.int32, (sub, tk), 1)

    def block(kb, diag):
        kv = kvt_ref[0, kb]
        la = jnp.concatenate([lat_ref[0, kb], ones], axis=1)
        nsb = rows // sub
        ts = [_dot(q_ref[0, 0, sb * sub:(sb + 1) * sub, :], kv) * MLA_SCALE_LOG2 for sb in range(nsb)]
        for sb in range(nsb):
            r = slice(sb * sub, (sb + 1) * sub)
            t = ts[sb]
            if diag:
                t = jnp.where(col <= lax.rem(rowi + sb * sub, tq) + (qi * tq - kb * tk), t, NEG_INF)
            m_prev = m_ref[r, :]
            m_new = jnp.maximum(m_prev, jnp.max(t, axis=-1, keepdims=True))
            p = jnp.exp2(t - _lanes(m_new, tk))
            alpha = jnp.exp2(m_prev - m_new)
            acc_ref[r, :] = _lanes(alpha, 2 * MLA_C) * acc_ref[r, :] + _dot(p.astype(BF16), la)
            m_ref[r, :] = m_new

    n_full = (qi * tq + 1) // tk
    n_all = (qi * tq + tq + tk - 1) // tk

    def full_body(kb, _):
        block(kb, False)
        return 0

    def diag_body(kb, _):
        block(kb, True)
        return 0

    lax.fori_loop(0, n_full, full_body, 0)
    lax.fori_loop(n_full, n_all, diag_body, 0)
    acc = acc_ref[...]
    o_ref[0, 0] = (acc[:, :MLA_C] / acc[:, MLA_C:]).astype(BF16)


def _mla_prompt(q, kvt, lat):
    b, nq, rows, dq = q.shape
    nk, tk = kvt.shape[1], kvt.shape[3]
    return pl.pallas_call(
        _mla_prompt_kernel,
        grid=(b, nq),
        in_specs=[pl.BlockSpec((1, 1, rows, dq), lambda i, j: (i, j, 0, 0)),
                  pl.BlockSpec((1, nk, dq, tk), lambda i, j: (i, 0, 0, 0)),
                  pl.BlockSpec((1, nk, tk, MLA_C), lambda i, j: (i, 0, 0, 0))],
        out_specs=pl.BlockSpec((1, 1, rows, MLA_C), lambda i, j: (i, j, 0, 0)),
        out_shape=jax.ShapeDtypeStruct((b, nq, rows, MLA_C), BF16),
        scratch_shapes=[pltpu.VMEM((rows, 2 * MLA_C), F32), pltpu.VMEM((rows, 128), F32)],
        compiler_params=_cparams("arbitrary", "arbitrary"),
        name="mla_prompt",
    )(q, kvt, lat)


def _mla_decode_kernel(layer, chunk, pt_ref, qa_ref, qr_ref, ln_ref, rn_ref, cl_ref, cr_ref, o_ref,
                       lbuf, rbuf, lsem, rsem):
    n_pages = lbuf.shape[1]
    slot = _paged_step(pt_ref, 0, n_pages, layer, [(cl_ref, lbuf, lsem), (cr_ref, rbuf, rsem)])
    qa = qa_ref[0]
    qr = qr_ref[0]
    rows = qa.shape[0]
    nt = rows // MLA_H
    t_row = lax.rem(lax.broadcasted_iota(jnp.int32, (rows, PAGE), 0), nt)
    mask = lax.broadcasted_iota(jnp.int32, (rows, PAGE), 1) <= t_row
    lats = [ln_ref[0]]
    krs = [rn_ref[0]]
    for c in range(n_pages // chunk):
        lats.append(lbuf[slot, c * chunk:(c + 1) * chunk].reshape(chunk * PAGE, MLA_C).astype(BF16))
        krs.append(jnp.concatenate([rbuf[slot, c * chunk + i] for i in range(chunk)], axis=1).astype(BF16))
    ts = [(_dot_nt(qa, lat) + _dot(qr, kr)) * MLA_SCALE_LOG2 for lat, kr in zip(lats, krs)]
    ts[0] = jnp.where(mask, ts[0], NEG_INF)
    m = functools.reduce(jnp.maximum, [jnp.max(t, axis=-1, keepdims=True) for t in ts])
    ps = [jnp.exp2(t - m) for t in ts]
    l = functools.reduce(jnp.add, [jnp.sum(p, axis=-1, keepdims=True) for p in ps])
    acc = functools.reduce(jnp.add, [_dot(p.astype(BF16), lat) for p, lat in zip(ps, lats)])
    o_ref[0] = (acc / l).astype(BF16)


def _mla_decode(layer, pt, qa, qr, ln, rn, cache_l, cache_r):
    bs, rows, _ = qa.shape
    n_pages = pt.shape[1]
    chunk = _row_tile(n_pages, 8)
    blk = lambda r, c: pl.BlockSpec((1, r, c), lambda i, pt: (i, 0, 0))
    return pl.pallas_call(
        functools.partial(_mla_decode_kernel, layer, chunk),
        grid_spec=pltpu.PrefetchScalarGridSpec(
            num_scalar_prefetch=1,
            grid=(bs,),
            in_specs=[blk(rows, MLA_C), blk(rows, MLA_ROPE), blk(PAGE, MLA_C), blk(MLA_ROPE, PAGE),
                      pl.BlockSpec(memory_space=pl.ANY), pl.BlockSpec(memory_space=pl.ANY)],
            out_specs=blk(rows, MLA_C),
            scratch_shapes=[pltpu.VMEM((2, n_pages, PAGE, MLA_C), F32), pltpu.VMEM((2, n_pages, MLA_ROPE, PAGE), F32),
                            pltpu.SemaphoreType.DMA((2,)), pltpu.SemaphoreType.DMA((2,))]),
        out_shape=jax.ShapeDtypeStruct((bs, rows, MLA_C), BF16),
        compiler_params=_cparams("arbitrary"),
        name="mla_decode",
    )(pt, qa, qr, ln, rn, cache_l, cache_r)


def _merge_kernel(x_ref, g_ref, ob_ref, ol_ref, gmix_ref, wg_ref, bg_ref, wa_ref, wb_ref, bdv_ref, wc_ref,
                  wo_ref, gmem_ref, wmq_ref, x1_ref, qm_ref):
    x = x_ref[...]
    xn = _rms(x, gmix_ref[...]).astype(BF16)
    gate = _sigmoid(_dot(xn, wg_ref[...]) + bg_ref[...])
    ya = _dot(g_ref[...], wa_ref[...])
    yb = _dot(ob_ref[...], wb_ref[...])
    yc = _dot(_dot(ol_ref[...], bdv_ref[...]).astype(BF16), wc_ref[...])
    merged = gate[:, :D_MODEL] * ya + gate[:, D_MODEL:2 * D_MODEL] * yb + gate[:, 2 * D_MODEL:] * yc
    x1 = x + _dot(merged.astype(BF16), wo_ref[...])
    x1_ref[...] = x1
    qm_ref[...] = _dot(_rms(x1, gmem_ref[...]).astype(BF16), wmq_ref[...]).astype(BF16)


def _merge(x, g, ob, ol, gmix, wg, bg, wa, wb, bdv, wc, wo, gmem, wmq):
    n = x.shape[0]
    tm = _row_tile(n, 256)
    row = lambda w: pl.BlockSpec((tm, w), lambda i: (i, 0))
    ws = [gmix, wg, bg, wa, wb, bdv, wc, wo, gmem, wmq]
    return pl.pallas_call(
        _merge_kernel,
        grid=(n // tm,),
        in_specs=[row(D_MODEL), row(LRU_W), row(512), row(1024)] + [_full(w.shape) for w in ws],
        out_specs=[row(D_MODEL), row(512)],
        out_shape=[jax.ShapeDtypeStruct((n, D_MODEL), F32), jax.ShapeDtypeStruct((n, 512), BF16)],
        compiler_params=_cparams("arbitrary"),
        name="merge",
    )(x, g, ob, ol, *ws)


def _memkv_kernel(m_ref, g_ref, w_ref, o_ref):
    o_ref[0] = _dot(_rms(m_ref[...], g_ref[0]).astype(BF16), w_ref[0])


def _memkv(mem, g_memkv, w_kv):
    nl, _, wout = w_kv.shape
    rows = mem.shape[0]
    return pl.pallas_call(
        _memkv_kernel,
        grid=(nl,),
        in_specs=[_full(mem.shape), pl.BlockSpec((1, 1, D_MODEL), lambda l: (l, 0, 0)),
                  pl.BlockSpec((1, D_MODEL, wout), lambda l: (l, 0, 0))],
        out_specs=pl.BlockSpec((1, rows, wout), lambda l: (l, 0, 0)),
        out_shape=jax.ShapeDtypeStruct((nl, rows, wout), F32),
        compiler_params=_cparams("arbitrary"),
        name="memkv",
    )(mem, g_memkv, w_kv)


def _mem_heads(q, mk, mv):
    outs = []
    for h in range(MEM_H):
        sl = slice(h * MEM_D, (h + 1) * MEM_D)
        s = _dot_nt(q[:, sl], mk[:, sl]) * MEM_SCALE
        e = jnp.exp(s - jnp.max(s, axis=-1, keepdims=True))
        p = e / jnp.sum(e, axis=-1, keepdims=True)
        outs.append(_dot(p.astype(BF16), mv[:, sl]))
    return jnp.concatenate(outs, axis=1).astype(BF16)


def _memattn_kernel(q_ref, mk_ref, mv_ref, o_ref):
    o_ref[0] = _mem_heads(q_ref[0], mk_ref[0].astype(BF16), mv_ref[0].astype(BF16))


def _memattn(q, mk, mv, tq):
    b, t, w = q.shape
    ml = mk.shape[1]
    return pl.pallas_call(
        _memattn_kernel,
        grid=(b, t // tq),
        in_specs=[pl.BlockSpec((1, tq, w), lambda i, j: (i, j, 0)),
                  pl.BlockSpec((1, ml, w), lambda i, j: (i, 0, 0)),
                  pl.BlockSpec((1, ml, w), lambda i, j: (i, 0, 0))],
        out_specs=pl.BlockSpec((1, tq, w), lambda i, j: (i, j, 0)),
        out_shape=jax.ShapeDtypeStruct((b, t, w), BF16),
        compiler_params=_cparams("arbitrary", "arbitrary"),
        name="memattn",
    )(q, mk, mv)


def _memattn_cache_kernel(q_ref, mk_ref, mv_ref, o_ref):
    q = q_ref[0]
    outs = []
    for h in range(MEM_H):
        s = _dot_nt(q[:, h * MEM_D:(h + 1) * MEM_D], mk_ref[0, 0, :, h, :].astype(BF16)) * MEM_SCALE
        e = jnp.exp(s - jnp.max(s, axis=-1, keepdims=True))
        p = e / jnp.sum(e, axis=-1, keepdims=True)
        outs.append(_dot(p.astype(BF16), mv_ref[0, 0, :, h, :].astype(BF16)))
    o_ref[0] = jnp.concatenate(outs, axis=1).astype(BF16)


def _memattn_cache(layer, q, cache_k, cache_v):
    bs, t, w = q.shape
    blk = (1, 1) + cache_k.shape[2:]
    return pl.pallas_call(
        _memattn_cache_kernel,
        grid=(bs,),
        in_specs=[pl.BlockSpec((1, t, w), lambda i: (i, 0, 0)),
                  pl.BlockSpec(blk, lambda i: (layer, i, 0, 0, 0)),
                  pl.BlockSpec(blk, lambda i: (layer, i, 0, 0, 0))],
        out_specs=pl.BlockSpec((1, t, w), lambda i: (i, 0, 0)),
        out_shape=jax.ShapeDtypeStruct((bs, t, w), BF16),
        compiler_params=_cparams("arbitrary"),
        name="memattn_cache",
    )(q, cache_k, cache_v)


def _mlp_kernel(final,x_ref, am_ref, wmo_ref, gmlp_ref, wup_ref, wdn_ref, gfin_ref, *rest):
    if final:
        o_ref, y_ref, x1_s, xn_s, acc_s = rest
    else:
        o_ref, x1_s, xn_s, acc_s = rest
    j = pl.program_id(1)

    @pl.when(j == 0)
    def _():
        x1 = x_ref[...] + _dot(am_ref[...], wmo_ref[...])
        x1_s[...] = x1
        xn_s[...] = _rms(x1, gmlp_ref[...]).astype(BF16)
        acc_s[...] = jnp.zeros_like(acc_s)

    h = jnp.maximum(_dot(xn_s[...], wup_ref[...]), 0.0)
    acc_s[...] += _dot((h * h).astype(BF16), wdn_ref[...])

    @pl.when(j == pl.num_programs(1) - 1)
    def _():
        out = x1_s[...] + acc_s[...]
        o_ref[...] = out
        if final:
            y_ref[...] = _rms(out, gfin_ref[...])


def _mlp(x, am, wmo, gmlp, wup, wdn, gfin, final):
    n = x.shape[0]
    tm = _row_tile(n, 512)
    tf = 1024
    row = lambda w: pl.BlockSpec((tm, w), lambda i, j: (i, 0))
    cst = lambda s: pl.BlockSpec(s, lambda i, j: (0, 0))
    n_out = 2 if final else 1
    outs = pl.pallas_call(
        functools.partial(_mlp_kernel, final),
        grid=(n // tm, D_FF // tf),
        in_specs=[row(D_MODEL), row(512), cst(wmo.shape), cst(gmlp.shape),
                  pl.BlockSpec((D_MODEL, tf), lambda i, j: (0, j)),
                  pl.BlockSpec((tf, D_MODEL), lambda i, j: (j, 0)), cst(gfin.shape)],
        out_specs=[row(D_MODEL)] * n_out,
        out_shape=[jax.ShapeDtypeStruct((n, D_MODEL), F32)] * n_out,
        scratch_shapes=[pltpu.VMEM((tm, D_MODEL), F32), pltpu.VMEM((tm, D_MODEL), BF16),
                        pltpu.VMEM((tm, D_MODEL), F32)],
        compiler_params=_cparams("arbitrary", "arbitrary"),
        name="mlp",
    )(x, am, wmo, gmlp, wup, wdn, gfin)
    return outs


def _rope_tables(pos):
    half = MLA_ROPE // 2
    inv = ROPE_THETA ** (-jnp.arange(half, dtype=F32) / half)
    ang = pos.astype(F32)[:, None] * inv[None, :]
    cos, sin = jnp.cos(ang), jnp.sin(ang)
    c = jnp.concatenate([cos, cos], axis=1)
    s = jnp.concatenate([-sin, sin], axis=1)
    pad = jnp.zeros((pos.shape[0], 128 - MLA_ROPE), F32)
    tabk = jnp.concatenate([c, pad, s, pad], axis=1)
    tabq = jnp.concatenate([jnp.tile(c, (1, MLA_H)), jnp.tile(s, (1, MLA_H))], axis=1)
    return tabk, tabq


def _swap_halves(w):
    half = w.shape[-1] // 2
    return jnp.concatenate([w[..., half:], w[..., :half]], axis=-1)


def _block_diag(w):
    n, a, b = w.shape
    return jnp.einsum('nab,nm->namb', w, jnp.eye(n, dtype=w.dtype)).reshape(n * a, n * b)


def kernel(x_prompt, x_sample, mem_prompt, cache_mla_latent, cache_mla_krope, cache_sb_k, cache_sb_v, cache_mem_k, cache_mem_v, state_lru_h, state_conv, page_table, g_mix, w_in, b_gate, conv_w, conv_b, lru_wa, lru_ba, lru_wx, lru_bx, lru_lambda, mla_gq, mla_wuq, mla_gkv, mla_wuk, mla_wuv, w_lru_o, w_sb_o, w_mla_o, w_out, g_mem, g_memkv, w_mq, w_mk, w_mv, w_mo, g_mlp, w_up, w_down, g_final):
    bp, tp, _ = x_prompt.shape
    bs, ts, _ = x_sample.shape
    depth = w_in.shape[0]
    n_pool = cache_sb_k.shape[1]
    n_pages = page_table.shape[1]
    past = n_pages * PAGE
    mem_len = mem_prompt.shape[1]
    n_p, n_s = bp * tp, bs * ts
    assert ts * SB_HEADS == 32 and ts * MLA_H == 32

    pos = jnp.concatenate([jnp.tile(jnp.arange(tp, dtype=jnp.int32), bp),
                           jnp.tile(past + jnp.arange(ts, dtype=jnp.int32), bs)])
    tabk, tabq = _rope_tables(pos)

    tq_sb = _row_tile(tp, SB_TQ)
    nq_sb = tp // tq_sb
    tq_mla = _row_tile(tp, MLA_TQ)
    tk_mla = _row_tile(tp, MLA_TK)
    nq_mla, nk_mla = tp // tq_mla, tp // tk_mla

    ck = cache_sb_k.transpose(0, 1, 3, 4, 2).reshape(depth, n_pool, SB_KVH * SB_D, PAGE)
    cv = cache_sb_v.transpose(0, 1, 3, 4, 2).reshape(depth, n_pool, SB_KVH * SB_D, PAGE)
    cr = cache_mla_krope.transpose(0, 1, 3, 2)

    mem_kv = _memkv(mem_prompt.reshape(bp * mem_len, D_MODEL), g_memkv[:, None, :],
                    jnp.concatenate([w_mk, w_mv], axis=2).astype(BF16))
    p_mk = mem_kv[:, :, :MEM_H * MEM_D].reshape(depth, bp, mem_len, MEM_H * MEM_D)
    p_mv = mem_kv[:, :, MEM_H * MEM_D:].reshape(depth, bp, mem_len, MEM_H * MEM_D)

    x = jnp.concatenate([x_prompt.reshape(n_p, D_MODEL), x_sample.reshape(n_s, D_MODEL)], axis=0)
    outs = {k: [] for k in ("p_lat", "p_kr", "p_k", "p_v", "p_h", "p_conv",
                            "s_lat", "s_kr", "s_k", "s_v", "s_h", "s_conv")}
    row2 = lambda v: v.reshape(1, -1)
    zpad = lambda a, n: jnp.pad(a, ((0, 0), (0, n - a.shape[1]), (0, 0)))
    y = None
    for l in range(depth):
        wl = w_in[l]
        w2 = wl[:, :2 * LRU_W].astype(BF16)
        kr_w = wl[:, 3200:3232]
        zc = jnp.zeros((D_MODEL, 128 - MLA_ROPE), F32)
        wp = jnp.concatenate([wl[:, 2048:3200], kr_w, zc, _swap_halves(kr_w), zc], axis=1).astype(BF16)
        wg = wl[:, 3232:].astype(BF16)
        wuq3 = mla_wuq[l].reshape(MLA_QR, MLA_H, MLA_NOPE + MLA_ROPE)
        rope_w = wuq3[:, :, MLA_NOPE:]
        wuq = jnp.concatenate([wuq3[:, :, :MLA_NOPE].reshape(MLA_QR, -1), rope_w.reshape(MLA_QR, -1),
                               _swap_halves(rope_w).reshape(MLA_QR, -1)], axis=1).astype(BF16)
        bdk = _block_diag(jnp.transpose(mla_wuk[l], (1, 2, 0))).astype(BF16)
        bdv = _block_diag(jnp.transpose(mla_wuv[l], (1, 0, 2))).astype(BF16)
        bda = _block_diag(lru_wa[l]).astype(BF16)
        bdx = _block_diag(lru_wx[l]).astype(BF16)
        gmix = row2(g_mix[l])
        lru_w = [gmix, w2, conv_w[l], row2(conv_b[l]), bda, row2(lru_ba[l]), bdx, row2(lru_bx[l]),
                 row2(lru_lambda[l])]

        qsb, k_new, v_new, lat_new, kr_new, qabs, qrope = _proj(
            x, gmix, wp, row2(mla_gq[l]), wuq, row2(mla_gkv[l]), bdk, tabk, tabq)

        g_p, h_p, conv_p = _lru_prompt(x[:n_p].reshape(bp, tp, D_MODEL), *lru_w)
        xs_tb = x[n_p:].reshape(bs, ts, D_MODEL).transpose(1, 0, 2).reshape(n_s, D_MODEL)
        tail = state_conv[l].transpose(1, 0, 2).reshape((CONV_W - 1) * bs, LRU_W)
        g_s, h_s, conv_s = _lru_sample(past, xs_tb, state_lru_h[l], tail, *lru_w)
        g_s = g_s.reshape(ts, bs, LRU_W).transpose(1, 0, 2).reshape(n_s, LRU_W)
        g_all = jnp.concatenate([g_p.reshape(n_p, LRU_W), g_s], axis=0)
        outs["p_h"].append(h_p.reshape(bp, LRU_W))
        outs["p_conv"].append(conv_p[:, 8 - (CONV_W - 1):, :])
        outs["s_h"].append(h_s)
        outs["s_conv"].append(conv_s.reshape(CONV_W - 1, bs, LRU_W).transpose(1, 0, 2))

        kb, vb = k_new.astype(BF16), v_new.astype(BF16)
        q5 = qsb[:n_p].reshape(bp, nq_sb, tq_sb, SB_KVH, SB_G, SB_D).transpose(0, 3, 1, 4, 2, 5)
        q5 = q5.reshape(bp, SB_KVH, nq_sb, SB_G * tq_sb, SB_D)
        k5 = kb[:n_p].reshape(bp, nq_sb, tq_sb, SB_KVH, SB_D)
        v5 = vb[:n_p].reshape(bp, nq_sb, tq_sb, SB_KVH, SB_D)
        ob_p = _sb_prompt(q5, k5.transpose(0, 3, 1, 4, 2), v5.transpose(0, 3, 1, 2, 4))
        ob_p = ob_p.reshape(bp, SB_KVH, nq_sb, SB_G, tq_sb, SB_D).transpose(0, 2, 4, 1, 3, 5)
        ob_p = ob_p.reshape(n_p, SB_HEADS * SB_D).astype(BF16)

        qs = qsb[n_p:].reshape(bs, ts, SB_KVH, SB_G, SB_D).transpose(0, 2, 3, 1, 4)
        q_bd = jnp.einsum('bhgtd,hk->bhgtkd', qs, jnp.eye(SB_KVH, dtype=BF16))
        q_bd = q_bd.reshape(bs, SB_HEADS * ts, SB_KVH * SB_D)
        kn = zpad(kb[n_p:].reshape(bs, ts, SB_KVH * SB_D), PAGE).transpose(0, 2, 1)
        vn = zpad(vb[n_p:].reshape(bs, ts, SB_KVH * SB_D), PAGE).transpose(0, 2, 1)
        ob_s = _sb_decode(l, page_table, q_bd, kn, vn, ck, cv)
        ob_s = ob_s.reshape(bs, SB_KVH, SB_G, ts, SB_KVH, SB_D)
        ob_s = jnp.stack([ob_s[:, h, :, :, h, :] for h in range(SB_KVH)], axis=1)
        ob_s = ob_s.transpose(0, 3, 1, 2, 4).reshape(n_s, SB_HEADS * SB_D).astype(BF16)
        ob = jnp.concatenate([ob_p, ob_s], axis=0)

        latb, krb = lat_new.astype(BF16), kr_new.astype(BF16)
        qa4 = qabs[:n_p].reshape(bp, nq_mla, tq_mla, MLA_H, MLA_C).transpose(0, 1, 3, 2, 4)
        qr4 = qrope[:n_p].reshape(bp, nq_mla, tq_mla, MLA_H, MLA_ROPE).transpose(0, 1, 3, 2, 4)
        qcat = jnp.concatenate([qa4, qr4, jnp.zeros(qa4.shape[:-1] + (256 - MLA_C - MLA_ROPE,), BF16)], axis=-1)
        qcat = qcat.reshape(bp, nq_mla, MLA_H * tq_mla, 256)
        lat4 = latb[:n_p].reshape(bp, nk_mla, tk_mla, MLA_C)
        kv4 = jnp.concatenate([lat4, krb[:n_p].reshape(bp, nk_mla, tk_mla, MLA_ROPE),
                               jnp.zeros((bp, nk_mla, tk_mla, 256 - MLA_C - MLA_ROPE), BF16)], axis=-1)
        ol_p = _mla_prompt(qcat, kv4.transpose(0, 1, 3, 2), lat4)
        ol_p = ol_p.reshape(bp, nq_mla, MLA_H, tq_mla, MLA_C).transpose(0, 1, 3, 2, 4).reshape(n_p, MLA_H * MLA_C)

        qa_s = qabs[n_p:].reshape(bs, ts, MLA_H, MLA_C).transpose(0, 2, 1, 3).reshape(bs, MLA_H * ts, MLA_C)
        qr_s = qrope[n_p:].reshape(bs, ts, MLA_H, MLA_ROPE).transpose(0, 2, 1, 3).reshape(bs, MLA_H * ts, MLA_ROPE)
        ln = zpad(latb[n_p:].reshape(bs, ts, MLA_C), PAGE)
        rn = zpad(krb[n_p:].reshape(bs, ts, MLA_ROPE), PAGE).transpose(0, 2, 1)
        ol_s = _mla_decode(l, page_table, qa_s, qr_s, ln, rn, cache_mla_latent, cr)
        ol_s = ol_s.reshape(bs, MLA_H, ts, MLA_C).transpose(0, 2, 1, 3).reshape(n_s, MLA_H * MLA_C)
        ol = jnp.concatenate([ol_p, ol_s], axis=0)

        outs["p_lat"].append(lat_new[:n_p].reshape(bp, tp, MLA_C))
        outs["p_kr"].append(kr_new[:n_p].reshape(bp, tp, MLA_ROPE))
        outs["p_k"].append(k_new[:n_p].reshape(bp, tp, SB_KVH, SB_D))
        outs["p_v"].append(v_new[:n_p].reshape(bp, tp, SB_KVH, SB_D))
        outs["s_lat"].append(lat_new[n_p:].reshape(bs, ts, MLA_C))
        outs["s_kr"].append(kr_new[n_p:].reshape(bs, ts, MLA_ROPE))
        outs["s_k"].append(k_new[n_p:].reshape(bs, ts, SB_KVH, SB_D))
        outs["s_v"].append(v_new[n_p:].reshape(bs, ts, SB_KVH, SB_D))

        x1, qm = _merge(x, g_all, ob, ol, gmix, wg, row2(b_gate[l]), w_lru_o[l].astype(BF16),
                        w_sb_o[l].astype(BF16), bdv, w_mla_o[l].astype(BF16), w_out[l].astype(BF16),
                        row2(g_mem[l]), w_mq[l].astype(BF16))
        am_p = _memattn(qm[:n_p].reshape(bp, tp, MEM_H * MEM_D), p_mk[l], p_mv[l], _row_tile(tp, 512))
        am_s = _memattn_cache(l, qm[n_p:].reshape(bs, ts, MEM_H * MEM_D), cache_mem_k, cache_mem_v)
        am = jnp.concatenate([am_p.reshape(n_p, -1), am_s.reshape(n_s, -1)], axis=0)
        final = l == depth - 1
        res = _mlp(x1, am, w_mo[l].astype(BF16), row2(g_mlp[l]), w_up[l].astype(BF16), w_down[l].astype(BF16),
                   row2(g_final), final)
        x = res[0]
        if final:
            y = res[1]

    st = {k: jnp.stack(v) for k, v in outs.items()}
    mshape = (depth, bp, mem_len, MEM_H, MEM_D)
    return (y[:n_p].reshape(bp, tp, D_MODEL), y[n_p:].reshape(bs, ts, D_MODEL),
            st["p_lat"], st["p_kr"], st["p_k"], st["p_v"], st["p_h"], st["p_conv"],
            p_mk.reshape(mshape), p_mv.reshape(mshape),
            st["s_lat"], st["s_kr"], st["s_k"], st["s_v"], st["s_h"], st["s_conv"])
```

```python
import functools

import numpy as np
import jax
import jax.numpy as jnp
from jax import lax
from jax.experimental import pallas as pl
from jax.experimental.pallas import tpu as pltpu

F32 = jnp.float32
BF16 = jnp.bfloat16

D_MODEL = 1024
PAGE = 128
LRU_W = 1024
LRU_BLOCKS = 16
CONV_W = 4
LRU_C = 8.0
SB_HEADS = 8
SB_KVH = 2
SB_G = 4
SB_D = 64
SB_SCALE = SB_D ** -0.5
MLA_H = 8
MLA_QR = 256
MLA_C = 128
MLA_NOPE = 64
MLA_ROPE = 32
MLA_V = 64
MLA_SCALE = (MLA_NOPE + MLA_ROPE) ** -0.5
ROPE_THETA = 10000.0
MEM_H = 4
MEM_D = 128
MEM_SCALE = MEM_D ** -0.5
D_FF = 4096
EPS = 1e-6

SB_TQ = 256
MLA_TQ = 512
MLA_TK = 512
SB_SUB = 1024
MLA_SUB = 512
SB_DEAD_MASS = 105.0
LOG2E = 1.4426950408889634

VMEM_LIMIT_V7X = 56 * 1024 * 1024
NEG_INF = float("-inf")


def _cparams(*sem):
    return pltpu.CompilerParams(dimension_semantics=sem, vmem_limit_bytes=VMEM_LIMIT_V7X)


def _rms(x, g):
    return x * lax.rsqrt(jnp.mean(x * x, axis=-1, keepdims=True) + EPS) * g


def _dot(a, b):
    return jnp.dot(a, b, preferred_element_type=F32)


def _dot_nt(a, b):
    return lax.dot_general(a, b, (((1,), (1,)), ((), ())), preferred_element_type=F32)


def _softplus(z):
    return jnp.maximum(z, 0.0) + jnp.log1p(jnp.exp(-jnp.abs(z)))


def _softplus_fast(z):
    return jnp.maximum(z, 0.0) + jnp.log(1.0 + jnp.exp(-jnp.abs(z)))


def _sigmoid(z):
    return 1.0 / (1.0 + jnp.exp(-z))


def _full(shape):
    n = len(shape)
    return pl.BlockSpec(shape, lambda *_: (0,) * n)


def _row_tile(n, want):
    t = min(want, n)
    while n % t:
        t //= 2
    return t


def _proj_kernel(x_ref, gmix_ref, wp_ref, gq_ref, wuq_ref, gkv_ref, bdk_ref, tabk_ref, tabq_ref,
                 qsb_ref, k_ref, v_ref, lat_ref, kr_ref, qabs_ref, qrope_ref):
    xn = _rms(x_ref[...], gmix_ref[...]).astype(BF16)
    u = _dot(xn, wp_ref[...])
    qsb_ref[...] = (u[:, 0:512] * SB_SCALE).astype(BF16)
    k_ref[...] = u[:, 512:640]
    v_ref[...] = u[:, 640:768]
    lat_ref[...] = _rms(u[:, 1024:1152], gkv_ref[...])
    tabk = tabk_ref[...]
    kr = u[:, 1152:1280] * tabk[:, :128] + u[:, 1280:1408] * tabk[:, 128:]
    kr_ref[...] = kr[:, :MLA_ROPE]
    cqn = _rms(u[:, 768:1024], gq_ref[...]).astype(BF16)
    qc = _dot(cqn, wuq_ref[...])
    qabs_ref[...] = _dot(qc[:, :512].astype(BF16), bdk_ref[...]).astype(BF16)
    tabq = tabq_ref[...]
    qrope_ref[...] = (qc[:, 512:768] * tabq[:, :256] + qc[:, 768:] * tabq[:, 256:]).astype(BF16)


def _proj(x, gmix, wp, gq, wuq, gkv, bdk, tabk, tabq):
    n = x.shape[0]
    tm = _row_tile(n, 512)
    row = lambda w: pl.BlockSpec((tm, w), lambda i: (i, 0))
    outs = [(512, BF16), (128, F32), (128, F32), (128, F32), (MLA_ROPE, F32), (1024, BF16), (256, BF16)]
    return pl.pallas_call(
        _proj_kernel,
        grid=(n // tm,),
        in_specs=[row(D_MODEL), _full(gmix.shape), _full(wp.shape), _full(gq.shape), _full(wuq.shape),
                  _full(gkv.shape), _full(bdk.shape), row(256), row(512)],
        out_specs=[row(w) for w, _ in outs],
        out_shape=[jax.ShapeDtypeStruct((n, w), dt) for w, dt in outs],
        compiler_params=_cparams("arbitrary"),
        name="proj",
    )(x, gmix, wp, gq, wuq, gkv, bdk, tabk, tabq)


def _lru_gates(xc, bda_ref, ba_ref, bdx_ref, bx_ref, lam_ref):
    xb = xc.astype(BF16)
    r = _sigmoid(_dot(xb, bda_ref[...]) + ba_ref[...])
    i = _sigmoid(_dot(xb, bdx_ref[...]) + bx_ref[...])
    log_a = -LRU_C * r * _softplus(-lam_ref[...])
    a = jnp.exp(log_a)
    mult = jnp.sqrt(jnp.maximum(1.0 - a * a, 0.0))
    return a, mult, i * xc


def _gelu_tanh(x):
    c = np.float32(np.sqrt(2.0 / np.pi))
    return 0.5 * x * (1.0 + jnp.tanh(c * (x + 0.044715 * (x * x * x))))


def _lru_prompt_kernel(x_ref, gmix_ref, w2_ref, cw_ref, cb_ref, bda_ref, ba_ref, bdx_ref, bx_ref, lam_ref,
                       g_ref, hlast_ref, conv_ref, ubuf, hc):
    t = pl.program_id(1)
    tc = x_ref.shape[1]

    @pl.when(t == 0)
    def _():
        ubuf[0:8, :] = jnp.zeros((8, LRU_W), F32)
        hc[...] = jnp.zeros_like(hc)

    xn = _rms(x_ref[0], gmix_ref[...]).astype(BF16)
    u = _dot(xn, w2_ref[...])
    ubuf[8:8 + tc, :] = u[:, :LRU_W]
    cw = cw_ref[...]
    xc = cb_ref[...]
    for k in range(CONV_W):
        xc = xc + ubuf[5 + k:5 + k + tc, :] * cw[k:k + 1, :]
    a, mult, ix = _lru_gates(xc, bda_ref, ba_ref, bdx_ref, bx_ref, lam_ref)
    rows = lax.broadcasted_iota(jnp.int32, (tc, LRU_W), 0)
    mult = jnp.where(rows == jnp.where(t == 0, 0, -1), 1.0, mult)
    b = mult * ix
    d = 1
    while d < tc:
        keep = rows >= d
        a_sh = jnp.where(keep, pltpu.roll(a, d, 0), 1.0)
        b_sh = jnp.where(keep, pltpu.roll(b, d, 0), 0.0)
        b = a * b_sh + b
        a = a * a_sh
        d *= 2
    h = a * hc[0:1, :] + b
    hc[0:1, :] = h[tc - 1:tc, :]
    hlast_ref[0] = h[tc - 1:tc, :]
    g_ref[0] = (h * _gelu_tanh(u[:, LRU_W:])).astype(BF16)
    tail = ubuf[tc:tc + 8, :]
    conv_ref[0] = tail
    ubuf[0:8, :] = tail


def _lru_prompt(xp, gmix, w2, cw, cb, bda, ba, bdx, bx, lam):
    b, t, _ = xp.shape
    tc = _row_tile(t, 256)
    assert tc >= 8
    ws = [gmix, w2, cw, cb, bda, ba, bdx, bx, lam]
    return pl.pallas_call(
        _lru_prompt_kernel,
        grid=(b, t // tc),
        in_specs=[pl.BlockSpec((1, tc, D_MODEL), lambda i, j: (i, j, 0))] + [_full(w.shape) for w in ws],
        out_specs=[pl.BlockSpec((1, tc, LRU_W), lambda i, j: (i, j, 0)),
                   pl.BlockSpec((1, 1, LRU_W), lambda i, j: (i, 0, 0)),
                   pl.BlockSpec((1, 8, LRU_W), lambda i, j: (i, 0, 0))],
        out_shape=[jax.ShapeDtypeStruct((b, t, LRU_W), BF16),
                   jax.ShapeDtypeStruct((b, 1, LRU_W), F32),
                   jax.ShapeDtypeStruct((b, 8, LRU_W), F32)],
        scratch_shapes=[pltpu.VMEM((tc + 8, LRU_W), F32), pltpu.VMEM((8, LRU_W), F32)],
        compiler_params=_cparams("arbitrary", "arbitrary"),
        name="lru_prompt",
    )(xp, *ws)


def _lru_sample_kernel(first_pos, x_ref, h0_ref, tail_ref, gmix_ref, w2_ref, cw_ref, cb_ref, bda_ref, ba_ref,
                       bdx_ref, bx_ref, lam_ref, g_ref, hlast_ref, conv_ref, ubuf):
    nb = h0_ref.shape[0]
    nt = x_ref.shape[0] // nb
    xn = _rms(x_ref[...], gmix_ref[...]).astype(BF16)
    u = _dot(xn, w2_ref[...])
    ubuf[0:(CONV_W - 1) * nb, :] = tail_ref[...]
    ubuf[(CONV_W - 1) * nb:, :] = u[:, :LRU_W]
    cw = cw_ref[...]
    xc = cb_ref[...]
    for k in range(CONV_W):
        xc = xc + ubuf[k * nb:(k + nt) * nb, :] * cw[k:k + 1, :]
    a, mult, ix = _lru_gates(xc, bda_ref, ba_ref, bdx_ref, bx_ref, lam_ref)
    gl = _gelu_tanh(u[:, LRU_W:])
    h = h0_ref[...]
    for t in range(nt):
        sl = slice(t * nb, (t + 1) * nb)
        m = mult[sl] if first_pos + t != 0 else jnp.ones_like(mult[sl])
        h = a[sl] * h + m * ix[sl]
        g_ref[sl, :] = (h * gl[sl]).astype(BF16)
    hlast_ref[...] = h
    conv_ref[...] = ubuf[nt * nb:(nt + CONV_W - 1) * nb, :]


def _lru_sample(first_pos, xs_tb, h0, tail, gmix, w2, cw, cb, bda, ba, bdx, bx, lam):
    n = xs_tb.shape[0]
    nb = h0.shape[0]
    args = [xs_tb, h0, tail, gmix, w2, cw, cb, bda, ba, bdx, bx, lam]
    return pl.pallas_call(
        functools.partial(_lru_sample_kernel, first_pos),
        grid=(1,),
        in_specs=[_full(a.shape) for a in args],
        out_specs=[_full((n, LRU_W)), _full((nb, LRU_W)), _full(((CONV_W - 1) * nb, LRU_W))],
        out_shape=[jax.ShapeDtypeStruct((n, LRU_W), BF16),
                   jax.ShapeDtypeStruct((nb, LRU_W), F32),
                   jax.ShapeDtypeStruct(((CONV_W - 1) * nb, LRU_W), F32)],
        scratch_shapes=[pltpu.VMEM((n + (CONV_W - 1) * nb, LRU_W), F32)],
        compiler_params=_cparams("arbitrary"),
        name="lru_sample",
    )(*args)


def _tri2(n):
    j = lax.rem(lax.broadcasted_iota(jnp.int32, (2 * n, n), 0), n)
    s = lax.broadcasted_iota(jnp.int32, (2 * n, n), 1)
    return jnp.where(j >= s, 1.0, 0.0).astype(BF16)


def _hi_lo(x):
    hi = x.astype(BF16)
    lo = (x - hi.astype(F32)).astype(BF16)
    return jnp.concatenate([hi, lo], axis=1)


def _lanes(x, n):
    return x if n == 128 else jnp.concatenate([x] * (n // 128), axis=1)


def _sb_weights(z, mask, tri2, carry):
    sp = _softplus_fast(z)
    if mask is not None:
        sp = jnp.where(mask, sp, 0.0)
    w = jnp.exp(z - _dot(_hi_lo(sp), tri2) - _lanes(carry, z.shape[1]))
    if mask is not None:
        w = jnp.where(mask, w, 0.0)
    return w, carry + jnp.sum(sp, axis=-1, keepdims=True)


def _sb_prompt_kernel(q_ref, kt_ref, v_ref, o_ref, acc_ref, carry_ref):
    qi = pl.program_id(2)
    rows = q_ref.shape[3]
    tk = kt_ref.shape[4]
    tq = rows // SB_G
    sub = min(SB_SUB, rows)
    tri2 = _tri2(tk)
    acc_ref[...] = jnp.zeros_like(acc_ref)
    carry_ref[...] = jnp.zeros_like(carry_ref)
    rowi = lax.broadcasted_iota(jnp.int32, (sub, tk), 0)
    col = lax.broadcasted_iota(jnp.int32, (sub, tk), 1)

    def block(kb, diag):
        kt = kt_ref[0, 0, kb]
        v = v_ref[0, 0, kb]
        for sb in range(rows // sub):
            r = slice(sb * sub, (sb + 1) * sub)
            z = _dot(q_ref[0, 0, 0, r, :], kt)
            mask = col < lax.rem(rowi + sb * sub, tq) if diag else None
            w, carry = _sb_weights(z, mask, tri2, carry_ref[r, :])
            carry_ref[r, :] = carry
            acc_ref[r, :] += _dot(w.astype(BF16), v)

    block(qi, True)

    def live():
        return jnp.min(carry_ref[...]) <= SB_DEAD_MASS

    def cond(st):
        return (st[0] < qi) & st[1]

    def body(st):
        block(qi - 1 - st[0], False)
        return st[0] + 1, live()

    lax.while_loop(cond, body, (jnp.int32(0), live()))
    o_ref[0, 0, 0] = acc_ref[...]


def _sb_prompt(q, kt, v):
    b, kvh, nq, rows, d = q.shape
    nk, tk = kt.shape[2], kt.shape[4]
    assert rows == SB_G * tk and nq == nk
    return pl.pallas_call(
        _sb_prompt_kernel,
        grid=(b, kvh, nq),
        in_specs=[pl.BlockSpec((1, 1, 1, rows, d), lambda i, h, j: (i, h, j, 0, 0)),
                  pl.BlockSpec((1, 1, nk, d, tk), lambda i, h, j: (i, h, 0, 0, 0)),
                  pl.BlockSpec((1, 1, nk, tk, d), lambda i, h, j: (i, h, 0, 0, 0))],
        out_specs=pl.BlockSpec((1, 1, 1, rows, d), lambda i, h, j: (i, h, j, 0, 0)),
        out_shape=jax.ShapeDtypeStruct(q.shape, F32),
        scratch_shapes=[pltpu.VMEM((rows, d), F32), pltpu.VMEM((rows, 128), F32)],
        compiler_params=_cparams("arbitrary", "arbitrary", "arbitrary"),
        name="sb_prompt",
    )(q, kt, v)


def _start_pages(pt_ref, b, first, count, layer, pairs, slot):
    def body(p, _):
        pg = pt_ref[b, first + p]
        for cache, buf, sem in pairs:
            pltpu.make_async_copy(cache.at[layer, pg], buf.at[slot, p], sem.at[slot]).start()
        return 0
    lax.fori_loop(0, count, body, 0)


def _wait_pages(count, layer, pairs, slot):
    def body(p, _):
        for cache, buf, sem in pairs:
            pltpu.make_async_copy(cache.at[layer, 0], buf.at[slot, p], sem.at[slot]).wait()
        return 0
    lax.fori_loop(0, count, body, 0)


def _paged_step(pt_ref, first, count, layer, pairs):
    b = pl.program_id(0)
    slot = lax.rem(b, 2)

    @pl.when(b == 0)
    def _():
        _start_pages(pt_ref, b, first, count, layer, pairs, slot)

    @pl.when(b + 1 < pl.num_programs(0))
    def _():
        _start_pages(pt_ref, b + 1, first, count, layer, pairs, 1 - slot)

    _wait_pages(count, layer, pairs, slot)
    return slot


def _sb_decode_kernel(layer, pt_ref, q_ref, kn_ref, vn_ref, ck_ref, cv_ref, o_ref,
                      kbuf, vbuf, kold, vold, ksem, vsem, osem):
    n_pages = pt_ref.shape[1]
    nb = n_pages // 2
    slot = _paged_step(pt_ref, n_pages - 2, 2, layer, [(ck_ref, kbuf, ksem), (cv_ref, vbuf, vsem)])
    seq = pl.program_id(0)
    q = q_ref[0]
    rows = q.shape[0]
    nt = rows // SB_HEADS
    tri2 = _tri2(2 * PAGE)
    t_row = lax.rem(lax.broadcasted_iota(jnp.int32, (rows, PAGE), 0), nt)
    mask = lax.broadcasted_iota(jnp.int32, (rows, PAGE), 1) < t_row
    w, carry = _sb_weights(_dot(q, kn_ref[0]), mask, _tri2(PAGE), jnp.zeros((rows, 128), F32))
    acc = _dot_nt(w.astype(BF16), vn_ref[0])

    def block(kpages, vpages, carry, acc):
        z = jnp.concatenate([_dot(q, k.astype(BF16)) for k in kpages], axis=1)
        w, carry = _sb_weights(z, None, tri2, carry)
        w = w.astype(BF16)
        for i in range(2):
            acc = acc + _dot_nt(w[:, i * PAGE:(i + 1) * PAGE], vpages[i].astype(BF16))
        return carry, acc

    carry, acc = block([kbuf[slot, 0], kbuf[slot, 1]], [vbuf[slot, 0], vbuf[slot, 1]], carry, acc)

    def cond(st):
        return (st[0] >= 0) & (jnp.min(st[1]) <= SB_DEAD_MASS)

    def body(st):
        blk, carry, acc = st
        copies = []
        for i in range(2):
            pg = pt_ref[seq, 2 * blk + i]
            for j, (cache, buf) in enumerate(((ck_ref, kold), (cv_ref, vold))):
                copies.append(pltpu.make_async_copy(cache.at[layer, pg], buf.at[i], osem.at[2 * i + j]))
        for cp in copies:
            cp.start()
        for cp in copies:
            cp.wait()
        carry, acc = block([kold[0], kold[1]], [vold[0], vold[1]], carry, acc)
        return blk - 1, carry, acc

    _, _, acc = lax.while_loop(cond, body, (jnp.int32(nb - 2), carry, acc))
    o_ref[0] = acc


def _sb_decode(layer, pt, q_bd, kn, vn, cache_k, cache_v):
    bs, rows, _ = q_bd.shape
    n_pages = pt.shape[1]
    assert n_pages % 2 == 0
    blk = lambda r, c: pl.BlockSpec((1, r, c), lambda i, pt: (i, 0, 0))
    page_buf = lambda *lead: pltpu.VMEM(lead + (PAGE, PAGE), F32)
    return pl.pallas_call(
        functools.partial(_sb_decode_kernel, layer),
        grid_spec=pltpu.PrefetchScalarGridSpec(
            num_scalar_prefetch=1,
            grid=(bs,),
            in_specs=[blk(rows, PAGE), blk(PAGE, PAGE), blk(PAGE, PAGE),
                      pl.BlockSpec(memory_space=pl.ANY), pl.BlockSpec(memory_space=pl.ANY)],
            out_specs=blk(rows, PAGE),
            scratch_shapes=[page_buf(2, 2), page_buf(2, 2), page_buf(2), page_buf(2),
                            pltpu.SemaphoreType.DMA((2,)), pltpu.SemaphoreType.DMA((2,)),
                            pltpu.SemaphoreType.DMA((4,))]),
        out_shape=jax.ShapeDtypeStruct((bs, rows, PAGE), F32),
        compiler_params=_cparams("arbitrary"),
        name="sb_decode",
    )(pt, q_bd, kn, vn, cache_k, cache_v)


MLA_SCALE_LOG2 = MLA_SCALE * LOG2E


def _mla_prompt_kernel(q_ref, kvt_ref, lat_ref, o_ref, acc_ref, m_ref):
    qi = pl.program_id(1)
    rows = q_ref.shape[2]
    tk = kvt_ref.shape[3]
    tq = rows // MLA_H
    sub = min(MLA_SUB, rows)
    acc_ref[...] = jnp.zeros_like(acc_ref)
    m_ref[...] = jnp.full_like(m_ref, NEG_INF)
    ones = jnp.ones((tk, MLA_C), BF16)
    rowi = lax.broadcasted_iota(jnp.int32, (sub, tk), 0)
    col = lax.broadcasted_iota(jnp.int32, (sub, tk), 1)

    def block(kb, diag):
        kv = kvt_ref[0, kb]
        la = jnp.concatenate([lat_ref[0, kb], ones], axis=1)
        nsb = rows // sub
        ts = [_dot(q_ref[0, 0, sb * sub:(sb + 1) * sub, :], kv) * MLA_SCALE_LOG2 for sb in range(nsb)]
        for sb in range(nsb):
            r = slice(sb * sub, (sb + 1) * sub)
            t = ts[sb]
            if diag:
                t = jnp.where(col <= lax.rem(rowi + sb * sub, tq) + (qi * tq - kb * tk), t, NEG_INF)
            m_prev = m_ref[r, :]
            m_new = jnp.maximum(m_prev, jnp.max(t, axis=-1, keepdims=True))
            p = jnp.exp2(t - _lanes(m_new, tk))
            alpha = jnp.exp2(m_prev - m_new)
            acc_ref[r, :] = _lanes(alpha, 2 * MLA_C) * acc_ref[r, :] + _dot(p.astype(BF16), la)
            m_ref[r, :] = m_new

    n_full = (qi * tq + 1) // tk
    n_all = (qi * tq + tq + tk - 1) // tk

    def full_body(kb, _):
        block(kb, False)
        return 0

    def diag_body(kb, _):
        block(kb, True)
        return 0

    lax.fori_loop(0, n_full, full_body, 0)
    lax.fori_loop(n_full, n_all, diag_body, 0)
    acc = acc_ref[...]
    o_ref[0, 0] = (acc[:, :MLA_C] / acc[:, MLA_C:]).astype(BF16)


def _mla_prompt(q, kvt, lat):
    b, nq, rows, dq = q.shape
    nk, tk = kvt.shape[1], kvt.shape[3]
    return pl.pallas_call(
        _mla_prompt_kernel,
        grid=(b, nq),
        in_specs=[pl.BlockSpec((1, 1, rows, dq), lambda i, j: (i, j, 0, 0)),
                  pl.BlockSpec((1, nk, dq, tk), lambda i, j: (i, 0, 0, 0)),
                  pl.BlockSpec((1, nk, tk, MLA_C), lambda i, j: (i, 0, 0, 0))],
        out_specs=pl.BlockSpec((1, 1, rows, MLA_C), lambda i, j: (i, j, 0, 0)),
        out_shape=jax.ShapeDtypeStruct((b, nq, rows, MLA_C), BF16),
        scratch_shapes=[pltpu.VMEM((rows, 2 * MLA_C), F32), pltpu.VMEM((rows, 128), F32)],
        compiler_params=_cparams("arbitrary", "arbitrary"),
        name="mla_prompt",
    )(q, kvt, lat)


def _mla_decode_kernel(layer, chunk, pt_ref, qa_ref, qr_ref, ln_ref, rn_ref, cl_ref, cr_ref, o_ref,
                       lbuf, rbuf, lsem, rsem):
    n_pages = lbuf.shape[1]
    slot = _paged_step(pt_ref, 0, n_pages, layer, [(cl_ref, lbuf, lsem), (cr_ref, rbuf, rsem)])
    qa = qa_ref[0]
    qr = qr_ref[0]
    rows = qa.shape[0]
    nt = rows // MLA_H
    t_row = lax.rem(lax.broadcasted_iota(jnp.int32, (rows, PAGE), 0), nt)
    mask = lax.broadcasted_iota(jnp.int32, (rows, PAGE), 1) <= t_row
    lats = [ln_ref[0]]
    krs = [rn_ref[0]]
    for c in range(n_pages // chunk):
        lats.append(lbuf[slot, c * chunk:(c + 1) * chunk].reshape(chunk * PAGE, MLA_C).astype(BF16))
        krs.append(jnp.concatenate([rbuf[slot, c * chunk + i] for i in range(chunk)], axis=1).astype(BF16))
    ts = [(_dot_nt(qa, lat) + _dot(qr, kr)) * MLA_SCALE_LOG2 for lat, kr in zip(lats, krs)]
    ts[0] = jnp.where(mask, ts[0], NEG_INF)
    m = functools.reduce(jnp.maximum, [jnp.max(t, axis=-1, keepdims=True) for t in ts])
    ps = [jnp.exp2(t - m) for t in ts]
    l = functools.reduce(jnp.add, [jnp.sum(p, axis=-1, keepdims=True) for p in ps])
    acc = functools.reduce(jnp.add, [_dot(p.astype(BF16), lat) for p, lat in zip(ps, lats)])
    o_ref[0] = (acc / l).astype(BF16)


def _mla_decode(layer, pt, qa, qr, ln, rn, cache_l, cache_r):
    bs, rows, _ = qa.shape
    n_pages = pt.shape[1]
    chunk = _row_tile(n_pages, 8)
    blk = lambda r, c: pl.BlockSpec((1, r, c), lambda i, pt: (i, 0, 0))
    return pl.pallas_call(
        functools.partial(_mla_decode_kernel, layer, chunk),
        grid_spec=pltpu.PrefetchScalarGridSpec(
            num_scalar_prefetch=1,
            grid=(bs,),
            in_specs=[blk(rows, MLA_C), blk(rows, MLA_ROPE), blk(PAGE, MLA_C), blk(MLA_ROPE, PAGE),
                      pl.BlockSpec(memory_space=pl.ANY), pl.BlockSpec(memory_space=pl.ANY)],
            out_specs=blk(rows, MLA_C),
            scratch_shapes=[pltpu.VMEM((2, n_pages, PAGE, MLA_C), F32), pltpu.VMEM((2, n_pages, MLA_ROPE, PAGE), F32),
                            pltpu.SemaphoreType.DMA((2,)), pltpu.SemaphoreType.DMA((2,))]),
        out_shape=jax.ShapeDtypeStruct((bs, rows, MLA_C), BF16),
        compiler_params=_cparams("arbitrary"),
        name="mla_decode",
    )(pt, qa, qr, ln, rn, cache_l, cache_r)


def _merge_kernel(x_ref, g_ref, ob_ref, ol_ref, gmix_ref, wg_ref, bg_ref, wa_ref, wb_ref, bdv_ref, wc_ref,
                  wo_ref, gmem_ref, wmq_ref, x1_ref, qm_ref):
    x = x_ref[...]
    xn = _rms(x, gmix_ref[...]).astype(BF16)
    gate = _sigmoid(_dot(xn, wg_ref[...]) + bg_ref[...])
    ya = _dot(g_ref[...], wa_ref[...])
    yb = _dot(ob_ref[...], wb_ref[...])
    yc = _dot(_dot(ol_ref[...], bdv_ref[...]).astype(BF16), wc_ref[...])
    merged = gate[:, :D_MODEL] * ya + gate[:, D_MODEL:2 * D_MODEL] * yb + gate[:, 2 * D_MODEL:] * yc
    x1 = x + _dot(merged.astype(BF16), wo_ref[...])
    x1_ref[...] = x1
    qm_ref[...] = _dot(_rms(x1, gmem_ref[...]).astype(BF16), wmq_ref[...]).astype(BF16)


def _merge(x, g, ob, ol, gmix, wg, bg, wa, wb, bdv, wc, wo, gmem, wmq):
    n = x.shape[0]
    tm = _row_tile(n, 256)
    row = lambda w: pl.BlockSpec((tm, w), lambda i: (i, 0))
    ws = [gmix, wg, bg, wa, wb, bdv, wc, wo, gmem, wmq]
    return pl.pallas_call(
        _merge_kernel,
        grid=(n // tm,),
        in_specs=[row(D_MODEL), row(LRU_W), row(512), row(1024)] + [_full(w.shape) for w in ws],
        out_specs=[row(D_MODEL), row(512)],
        out_shape=[jax.ShapeDtypeStruct((n, D_MODEL), F32), jax.ShapeDtypeStruct((n, 512), BF16)],
        compiler_params=_cparams("arbitrary"),
        name="merge",
    )(x, g, ob, ol, *ws)


def _memkv_kernel(m_ref, g_ref, w_ref, o_ref):
    o_ref[0] = _dot(_rms(m_ref[...], g_ref[0]).astype(BF16), w_ref[0])


def _memkv(mem, g_memkv, w_kv):
    nl, _, wout = w_kv.shape
    rows = mem.shape[0]
    return pl.pallas_call(
        _memkv_kernel,
        grid=(nl,),
        in_specs=[_full(mem.shape), pl.BlockSpec((1, 1, D_MODEL), lambda l: (l, 0, 0)),
                  pl.BlockSpec((1, D_MODEL, wout), lambda l: (l, 0, 0))],
        out_specs=pl.BlockSpec((1, rows, wout), lambda l: (l, 0, 0)),
        out_shape=jax.ShapeDtypeStruct((nl, rows, wout), F32),
        compiler_params=_cparams("arbitrary"),
        name="memkv",
    )(mem, g_memkv, w_kv)


def _mem_heads(q, mk, mv):
    outs = []
    for h in range(MEM_H):
        sl = slice(h * MEM_D, (h + 1) * MEM_D)
        s = _dot_nt(q[:, sl], mk[:, sl]) * MEM_SCALE
        e = jnp.exp(s - jnp.max(s, axis=-1, keepdims=True))
        p = e / jnp.sum(e, axis=-1, keepdims=True)
        outs.append(_dot(p.astype(BF16), mv[:, sl]))
    return jnp.concatenate(outs, axis=1).astype(BF16)


def _memattn_kernel(q_ref, mk_ref, mv_ref, o_ref):
    o_ref[0] = _mem_heads(q_ref[0], mk_ref[0].astype(BF16), mv_ref[0].astype(BF16))


def _memattn(q, mk, mv, tq):
    b, t, w = q.shape
    ml = mk.shape[1]
    return pl.pallas_call(
        _memattn_kernel,
        grid=(b, t // tq),
        in_specs=[pl.BlockSpec((1, tq, w), lambda i, j: (i, j, 0)),
                  pl.BlockSpec((1, ml, w), lambda i, j: (i, 0, 0)),
                  pl.BlockSpec((1, ml, w), lambda i, j: (i, 0, 0))],
        out_specs=pl.BlockSpec((1, tq, w), lambda i, j: (i, j, 0)),
        out_shape=jax.ShapeDtypeStruct((b, t, w), BF16),
        compiler_params=_cparams("arbitrary", "arbitrary"),
        name="memattn",
    )(q, mk, mv)


def _memattn_cache_kernel(q_ref, mk_ref, mv_ref, o_ref):
    q = q_ref[0]
    outs = []
    for h in range(MEM_H):
        s = _dot_nt(q[:, h * MEM_D:(h + 1) * MEM_D], mk_ref[0, 0, :, h, :].astype(BF16)) * MEM_SCALE
        e = jnp.exp(s - jnp.max(s, axis=-1, keepdims=True))
        p = e / jnp.sum(e, axis=-1, keepdims=True)
        outs.append(_dot(p.astype(BF16), mv_ref[0, 0, :, h, :].astype(BF16)))
    o_ref[0] = jnp.concatenate(outs, axis=1).astype(BF16)


def _memattn_cache(layer, q, cache_k, cache_v):
    bs, t, w = q.shape
    blk = (1, 1) + cache_k.shape[2:]
    return pl.pallas_call(
        _memattn_cache_kernel,
        grid=(bs,),
        in_specs=[pl.BlockSpec((1, t, w), lambda i: (i, 0, 0)),
                  pl.BlockSpec(blk, lambda i: (layer, i, 0, 0, 0)),
                  pl.BlockSpec(blk, lambda i: (layer, i, 0, 0, 0))],
        out_specs=pl.BlockSpec((1, t, w), lambda i: (i, 0, 0)),
        out_shape=jax.ShapeDtypeStruct((bs, t, w), BF16),
        compiler_params=_cparams("arbitrary"),
        name="memattn_cache",
    )(q, cache_k, cache_v)


def _mlp_kernel(final,x_ref, am_ref, wmo_ref, gmlp_ref, wup_ref, wdn_ref, gfin_ref, *rest):
    if final:
        o_ref, y_ref, x1_s, xn_s, acc_s = rest
    else:
        o_ref, x1_s, xn_s, acc_s = rest
    j = pl.program_id(1)

    @pl.when(j == 0)
    def _():
        x1 = x_ref[...] + _dot(am_ref[...], wmo_ref[...])
        x1_s[...] = x1
        xn_s[...] = _rms(x1, gmlp_ref[...]).astype(BF16)
        acc_s[...] = jnp.zeros_like(acc_s)

    h = jnp.maximum(_dot(xn_s[...], wup_ref[...]), 0.0)
    acc_s[...] += _dot((h * h).astype(BF16), wdn_ref[...])

    @pl.when(j == pl.num_programs(1) - 1)
    def _():
        out = x1_s[...] + acc_s[...]
        o_ref[...] = out
        if final:
            y_ref[...] = _rms(out, gfin_ref[...])


def _mlp(x, am, wmo, gmlp, wup, wdn, gfin, final):
    n = x.shape[0]
    tm = _row_tile(n, 512)
    tf = 1024
    row = lambda w: pl.BlockSpec((tm, w), lambda i, j: (i, 0))
    cst = lambda s: pl.BlockSpec(s, lambda i, j: (0, 0))
    n_out = 2 if final else 1
    outs = pl.pallas_call(
        functools.partial(_mlp_kernel, final),
        grid=(n // tm, D_FF // tf),
        in_specs=[row(D_MODEL), row(512), cst(wmo.shape), cst(gmlp.shape),
                  pl.BlockSpec((D_MODEL, tf), lambda i, j: (0, j)),
                  pl.BlockSpec((tf, D_MODEL), lambda i, j: (j, 0)), cst(gfin.shape)],
        out_specs=[row(D_MODEL)] * n_out,
        out_shape=[jax.ShapeDtypeStruct((n, D_MODEL), F32)] * n_out,
        scratch_shapes=[pltpu.VMEM((tm, D_MODEL), F32), pltpu.VMEM((tm, D_MODEL), BF16),
                        pltpu.VMEM((tm, D_MODEL), F32)],
        compiler_params=_cparams("arbitrary", "arbitrary"),
        name="mlp",
    )(x, am, wmo, gmlp, wup, wdn, gfin)
    return outs


def _rope_tables(pos):
    half = MLA_ROPE // 2
    inv = ROPE_THETA ** (-jnp.arange(half, dtype=F32) / half)
    ang = pos.astype(F32)[:, None] * inv[None, :]
    cos, sin = jnp.cos(ang), jnp.sin(ang)
    c = jnp.concatenate([cos, cos], axis=1)
    s = jnp.concatenate([-sin, sin], axis=1)
    pad = jnp.zeros((pos.shape[0], 128 - MLA_ROPE), F32)
    tabk = jnp.concatenate([c, pad, s, pad], axis=1)
    tabq = jnp.concatenate([jnp.tile(c, (1, MLA_H)), jnp.tile(s, (1, MLA_H))], axis=1)
    return tabk, tabq


def _swap_halves(w):
    half = w.shape[-1] // 2
    return jnp.concatenate([w[..., half:], w[..., :half]], axis=-1)


def _block_diag(w):
    n, a, b = w.shape
    return jnp.einsum('nab,nm->namb', w, jnp.eye(n, dtype=w.dtype)).reshape(n * a, n * b)


def kernel(x_prompt, x_sample, mem_prompt, cache_mla_latent, cache_mla_krope, cache_sb_k, cache_sb_v, cache_mem_k, cache_mem_v, state_lru_h, state_conv, page_table, g_mix, w_in, b_gate, conv_w, conv_b, lru_wa, lru_ba, lru_wx, lru_bx, lru_lambda, mla_gq, mla_wuq, mla_gkv, mla_wuk, mla_wuv, w_lru_o, w_sb_o, w_mla_o, w_out, g_mem, g_memkv, w_mq, w_mk, w_mv, w_mo, g_mlp, w_up, w_down, g_final):
    bp, tp, _ = x_prompt.shape
    bs, ts, _ = x_sample.shape
    depth = w_in.shape[0]
    n_pool = cache_sb_k.shape[1]
    n_pages = page_table.shape[1]
    past = n_pages * PAGE
    mem_len = mem_prompt.shape[1]
    n_p, n_s = bp * tp, bs * ts
    assert ts * SB_HEADS == 32 and ts * MLA_H == 32

    pos = jnp.concatenate([jnp.tile(jnp.arange(tp, dtype=jnp.int32), bp),
                           jnp.tile(past + jnp.arange(ts, dtype=jnp.int32), bs)])
    tabk, tabq = _rope_tables(pos)

    tq_sb = _row_tile(tp, SB_TQ)
    nq_sb = tp // tq_sb
    tq_mla = _row_tile(tp, MLA_TQ)
    tk_mla = _row_tile(tp, MLA_TK)
    nq_mla, nk_mla = tp // tq_mla, tp // tk_mla

    ck = cache_sb_k.transpose(0, 1, 3, 4, 2).reshape(depth, n_pool, SB_KVH * SB_D, PAGE)
    cv = cache_sb_v.transpose(0, 1, 3, 4, 2).reshape(depth, n_pool, SB_KVH * SB_D, PAGE)
    cr = cache_mla_krope.transpose(0, 1, 3, 2)

    mem_kv = _memkv(mem_prompt.reshape(bp * mem_len, D_MODEL), g_memkv[:, None, :],
                    jnp.concatenate([w_mk, w_mv], axis=2).astype(BF16))
    p_mk = mem_kv[:, :, :MEM_H * MEM_D].reshape(depth, bp, mem_len, MEM_H * MEM_D)
    p_mv = mem_kv[:, :, MEM_H * MEM_D:].reshape(depth, bp, mem_len, MEM_H * MEM_D)

    x = jnp.concatenate([x_prompt.reshape(n_p, D_MODEL), x_sample.reshape(n_s, D_MODEL)], axis=0)
    outs = {k: [] for k in ("p_lat", "p_kr", "p_k", "p_v", "p_h", "p_conv",
                            "s_lat", "s_kr", "s_k", "s_v", "s_h", "s_conv")}
    row2 = lambda v: v.reshape(1, -1)
    zpad = lambda a, n: jnp.pad(a, ((0, 0), (0, n - a.shape[1]), (0, 0)))
    y = None
    for l in range(depth):
        wl = w_in[l]
        w2 = wl[:, :2 * LRU_W].astype(BF16)
        kr_w = wl[:, 3200:3232]
        zc = jnp.zeros((D_MODEL, 128 - MLA_ROPE), F32)
        wp = jnp.concatenate([wl[:, 2048:3200], kr_w, zc, _swap_halves(kr_w), zc], axis=1).astype(BF16)
        wg = wl[:, 3232:].astype(BF16)
        wuq3 = mla_wuq[l].reshape(MLA_QR, MLA_H, MLA_NOPE + MLA_ROPE)
        rope_w = wuq3[:, :, MLA_NOPE:]
        wuq = jnp.concatenate([wuq3[:, :, :MLA_NOPE].reshape(MLA_QR, -1), rope_w.reshape(MLA_QR, -1),
                               _swap_halves(rope_w).reshape(MLA_QR, -1)], axis=1).astype(BF16)
        bdk = _block_diag(jnp.transpose(mla_wuk[l], (1, 2, 0))).astype(BF16)
        bdv = _block_diag(jnp.transpose(mla_wuv[l], (1, 0, 2))).astype(BF16)
        bda = _block_diag(lru_wa[l]).astype(BF16)
        bdx = _block_diag(lru_wx[l]).astype(BF16)
        gmix = row2(g_mix[l])
        lru_w = [gmix, w2, conv_w[l], row2(conv_b[l]), bda, row2(lru_ba[l]), bdx, row2(lru_bx[l]),
                 row2(lru_lambda[l])]

        qsb, k_new, v_new, lat_new, kr_new, qabs, qrope = _proj(
            x, gmix, wp, row2(mla_gq[l]), wuq, row2(mla_gkv[l]), bdk, tabk, tabq)

        g_p, h_p, conv_p = _lru_prompt(x[:n_p].reshape(bp, tp, D_MODEL), *lru_w)
        xs_tb = x[n_p:].reshape(bs, ts, D_MODEL).transpose(1, 0, 2).reshape(n_s, D_MODEL)
        tail = state_conv[l].transpose(1, 0, 2).reshape((CONV_W - 1) * bs, LRU_W)
        g_s, h_s, conv_s = _lru_sample(past, xs_tb, state_lru_h[l], tail, *lru_w)
        g_s = g_s.reshape(ts, bs, LRU_W).transpose(1, 0, 2).reshape(n_s, LRU_W)
        g_all = jnp.concatenate([g_p.reshape(n_p, LRU_W), g_s], axis=0)
        outs["p_h"].append(h_p.reshape(bp, LRU_W))
        outs["p_conv"].append(conv_p[:, 8 - (CONV_W - 1):, :])
        outs["s_h"].append(h_s)
        outs["s_conv"].append(conv_s.reshape(CONV_W - 1, bs, LRU_W).transpose(1, 0, 2))

        kb, vb = k_new.astype(BF16), v_new.astype(BF16)
        q5 = qsb[:n_p].reshape(bp, nq_sb, tq_sb, SB_KVH, SB_G, SB_D).transpose(0, 3, 1, 4, 2, 5)
        q5 = q5.reshape(bp, SB_KVH, nq_sb, SB_G * tq_sb, SB_D)
        k5 = kb[:n_p].reshape(bp, nq_sb, tq_sb, SB_KVH, SB_D)
        v5 = vb[:n_p].reshape(bp, nq_sb, tq_sb, SB_KVH, SB_D)
        ob_p = _sb_prompt(q5, k5.transpose(0, 3, 1, 4, 2), v5.transpose(0, 3, 1, 2, 4))
        ob_p = ob_p.reshape(bp, SB_KVH, nq_sb, SB_G, tq_sb, SB_D).transpose(0, 2, 4, 1, 3, 5)
        ob_p = ob_p.reshape(n_p, SB_HEADS * SB_D).astype(BF16)

        qs = qsb[n_p:].reshape(bs, ts, SB_KVH, SB_G, SB_D).transpose(0, 2, 3, 1, 4)
        q_bd = jnp.einsum('bhgtd,hk->bhgtkd', qs, jnp.eye(SB_KVH, dtype=BF16))
        q_bd = q_bd.reshape(bs, SB_HEADS * ts, SB_KVH * SB_D)
        kn = zpad(kb[n_p:].reshape(bs, ts, SB_KVH * SB_D), PAGE).transpose(0, 2, 1)
        vn = zpad(vb[n_p:].reshape(bs, ts, SB_KVH * SB_D), PAGE).transpose(0, 2, 1)
        ob_s = _sb_decode(l, page_table, q_bd, kn, vn, ck, cv)
        ob_s = ob_s.reshape(bs, SB_KVH, SB_G, ts, SB_KVH, SB_D)
        ob_s = jnp.stack([ob_s[:, h, :, :, h, :] for h in range(SB_KVH)], axis=1)
        ob_s = ob_s.transpose(0, 3, 1, 2, 4).reshape(n_s, SB_HEADS * SB_D).astype(BF16)
        ob = jnp.concatenate([ob_p, ob_s], axis=0)

        latb, krb = lat_new.astype(BF16), kr_new.astype(BF16)
        qa4 = qabs[:n_p].reshape(bp, nq_mla, tq_mla, MLA_H, MLA_C).transpose(0, 1, 3, 2, 4)
        qr4 = qrope[:n_p].reshape(bp, nq_mla, tq_mla, MLA_H, MLA_ROPE).transpose(0, 1, 3, 2, 4)
        qcat = jnp.concatenate([qa4, qr4, jnp.zeros(qa4.shape[:-1] + (256 - MLA_C - MLA_ROPE,), BF16)], axis=-1)
        qcat = qcat.reshape(bp, nq_mla, MLA_H * tq_mla, 256)
        lat4 = latb[:n_p].reshape(bp, nk_mla, tk_mla, MLA_C)
        kv4 = jnp.concatenate([lat4, krb[:n_p].reshape(bp, nk_mla, tk_mla, MLA_ROPE),
                               jnp.zeros((bp, nk_mla, tk_mla, 256 - MLA_C - MLA_ROPE), BF16)], axis=-1)
        ol_p = _mla_prompt(qcat, kv4.transpose(0, 1, 3, 2), lat4)
        ol_p = ol_p.reshape(bp, nq_mla, MLA_H, tq_mla, MLA_C).transpose(0, 1, 3, 2, 4).reshape(n_p, MLA_H * MLA_C)

        qa_s = qabs[n_p:].reshape(bs, ts, MLA_H, MLA_C).transpose(0, 2, 1, 3).reshape(bs, MLA_H * ts, MLA_C)
        qr_s = qrope[n_p:].reshape(bs, ts, MLA_H, MLA_ROPE).transpose(0, 2, 1, 3).reshape(bs, MLA_H * ts, MLA_ROPE)
        ln = zpad(latb[n_p:].reshape(bs, ts, MLA_C), PAGE)
        rn = zpad(krb[n_p:].reshape(bs, ts, MLA_ROPE), PAGE).transpose(0, 2, 1)
        ol_s = _mla_decode(l, page_table, qa_s, qr_s, ln, rn, cache_mla_latent, cr)
        ol_s = ol_s.reshape(bs, MLA_H, ts, MLA_C).transpose(0, 2, 1, 3).reshape(n_s, MLA_H * MLA_C)
        ol = jnp.concatenate([ol_p, ol_s], axis=0)

        outs["p_lat"].append(lat_new[:n_p].reshape(bp, tp, MLA_C))
        outs["p_kr"].append(kr_new[:n_p].reshape(bp, tp, MLA_ROPE))
        outs["p_k"].append(k_new[:n_p].reshape(bp, tp, SB_KVH, SB_D))
        outs["p_v"].append(v_new[:n_p].reshape(bp, tp, SB_KVH, SB_D))
        outs["s_lat"].append(lat_new[n_p:].reshape(bs, ts, MLA_C))
        outs["s_kr"].append(kr_new[n_p:].reshape(bs, ts, MLA_ROPE))
        outs["s_k"].append(k_new[n_p:].reshape(bs, ts, SB_KVH, SB_D))
        outs["s_v"].append(v_new[n_p:].reshape(bs, ts, SB_KVH, SB_D))

        x1, qm = _merge(x, g_all, ob, ol, gmix, wg, row2(b_gate[l]), w_lru_o[l].astype(BF16),
                        w_sb_o[l].astype(BF16), bdv, w_mla_o[l].astype(BF16), w_out[l].astype(BF16),
                        row2(g_mem[l]), w_mq[l].astype(BF16))
        am_p = _memattn(qm[:n_p].reshape(bp, tp, MEM_H * MEM_D), p_mk[l], p_mv[l], _row_tile(tp, 512))
        am_s = _memattn_cache(l, qm[n_p:].reshape(bs, ts, MEM_H * MEM_D), cache_mem_k, cache_mem_v)
        am = jnp.concatenate([am_p.reshape(n_p, -1), am_s.reshape(n_s, -1)], axis=0)
        final = l == depth - 1
        res = _mlp(x1, am, w_mo[l].astype(BF16), row2(g_mlp[l]), w_up[l].astype(BF16), w_down[l].astype(BF16),
                   row2(g_final), final)
        x = res[0]
        if final:
            y = res[1]

    st = {k: jnp.stack(v) for k, v in outs.items()}
    mshape = (depth, bp, mem_len, MEM_H, MEM_D)
    return (y[:n_p].reshape(bp, tp, D_MODEL), y[n_p:].reshape(bs, ts, D_MODEL),
            st["p_lat"], st["p_kr"], st["p_k"], st["p_v"], st["p_h"], st["p_conv"],
            p_mk.reshape(mshape), p_mv.reshape(mshape),
            st["s_lat"], st["s_kr"], st["s_k"], st["s_v"], st["s_h"], st["s_conv"])
```
